```python
import math
import jax
import jax.numpy as jnp
from jax import lax
import numpy as np

D_MODEL = 1024
BATCH = 8
SEQ = 2048
DEPTH = 2
DEC_BATCH = 128
DEC_SEQ = 1
PAST_LEN = 8192
PAGE_SIZE = 128

N_A = (DEPTH + 1) // 2
N_C = DEPTH // 2
EPS = 1e-6
NEG = -1e30

POOL_WIDTH = D_MODEL // 2
POOL_GROUPS = 4
POOL_GROUP_DIM = POOL_WIDTH // POOL_GROUPS
POOL_WINDOWS = (2, 4, 8, 16)
POOL_STATE = max(POOL_WINDOWS) - 1

MLA_HEADS = 8
MLA_NOPE = 64
MLA_ROPE = 32
MLA_QK = MLA_NOPE + MLA_ROPE
MLA_V = 64
MLA_Q_LORA = 384
MLA_KV_LORA = 256
MLA_THETA = 10000.0
MLA_SCALE = MLA_QK ** -0.5
MLA_Q_BLOCK = 128
A_IN = POOL_WIDTH + MLA_Q_LORA + MLA_KV_LORA + MLA_ROPE
A_MIX = POOL_WIDTH + MLA_HEADS * MLA_V

C_HEADS = 16
C_KV_HEADS = 4
C_GROUP = C_HEADS // C_KV_HEADS
C_HEAD_DIM = 64
C_ROT = C_HEAD_DIM // 4
C_THETA = 500000.0
C_SCALE = C_HEAD_DIM ** -0.5
MOBA_BLOCK = 256
MOBA_TOPK = 3
MOBA_Q_CHUNK = 16
C_IN = (C_HEADS + 2 * C_KV_HEADS) * C_HEAD_DIM
C_MIX = C_HEADS * C_HEAD_DIM

FFN_HIDDEN = -(-(8 * D_MODEL) // (3 * 256)) * 256

kernel_name = 'hybrid_pool_mla_moba_decode_step'


def rms_norm(x, g):
    xf = x.astype(jnp.float32)
    y = xf * lax.rsqrt(jnp.mean(xf * xf, axis=-1, keepdims=True) + EPS)
    return (y * g.astype(jnp.float32)).astype(x.dtype)


def rope_angles(pos, dim, theta):
    inv_freq = jnp.exp(jnp.arange(0, dim, 2, dtype=jnp.float32) * (-math.log(theta) / dim))
    ang = pos.astype(jnp.float32)[:, None] * inv_freq[None, :]
    return jnp.cos(ang), jnp.sin(ang)


def apply_rope(x, cos, sin):
    if x.ndim == 4:
        cos, sin = cos[:, None, :], sin[:, None, :]
    x1, x2 = jnp.split(x.astype(jnp.float32), 2, axis=-1)
    return jnp.concatenate([x1 * cos - x2 * sin, x2 * cos + x1 * sin], axis=-1).astype(x.dtype)


def gather_pages(pool, layer, page_table):
    rows = pool[layer, page_table]
    return rows.reshape(page_table.shape[0], -1, *pool.shape[3:])


def sweep_queries(fn, qs, qpos, block):
    S = qpos.shape[0]
    if S <= block or S % block:
        return fn(tuple(qs) + (qpos,))
    nb = S // block

    def split(a):
        return jnp.moveaxis(a.reshape(a.shape[0], nb, block, *a.shape[2:]), 1, 0)

    out = lax.map(fn, tuple(split(a) for a in qs) + (qpos.reshape(nb, block),))
    out = jnp.moveaxis(out, 0, 1)
    return out.reshape(out.shape[0], S, *out.shape[3:])


def pool_mix(u_full, n_new, w_pool, s_pool):
    B, L, _ = u_full.shape
    uf = u_full.astype(jnp.float32)
    cs = jnp.concatenate([jnp.zeros((B, 1, POOL_WIDTH), jnp.float32), jnp.cumsum(uf, axis=1)], axis=1)
    t = jnp.arange(L - n_new, L)
    means = []
    for g, w in enumerate(POOL_WINDOWS):
        sl = slice(g * POOL_GROUP_DIM, (g + 1) * POOL_GROUP_DIM)
        start = jnp.maximum(t + 1 - w, 0)
        win_sum = cs[:, t + 1, sl] - cs[:, start, sl]
        means.append(win_sum / (t + 1 - start).astype(jnp.float32)[None, :, None])
    pooled = jnp.stack(means, axis=2)
    diff = pooled - uf[:, L - n_new:].reshape(B, n_new, POOL_GROUPS, POOL_GROUP_DIM)
    y = jnp.einsum('bngc,gcd->bngd', diff, w_pool.astype(jnp.float32)).reshape(B, n_new, POOL_WIDTH)
    return (y * s_pool.astype(jnp.float32)).astype(u_full.dtype)


def even_layer(x, pos, past_ckv, past_krope, past_pool, g_mix, w_in, g_q_lora, g_kv_lora,
               w_uq, w_uk, w_uv, g_q, g_k, w_pool, s_pool, w_out):
    B, S, _ = x.shape
    h = rms_norm(x, g_mix)
    u, c_q, c_kv, k_rope = jnp.split(
        h @ w_in, [POOL_WIDTH, POOL_WIDTH + MLA_Q_LORA, POOL_WIDTH + MLA_Q_LORA + MLA_KV_LORA], axis=-1)
    ckv_new = rms_norm(c_kv, g_kv_lora)

    u_full = u if past_pool is None else jnp.concatenate([past_pool.astype(u.dtype), u], axis=1)
    pool_out = pool_mix(u_full, S, w_pool, s_pool)
    pool_state = u_full[:, -POOL_STATE:]

    q = jnp.einsum('bsr,rhd->bshd', rms_norm(c_q, g_q_lora), w_uq)
    q = rms_norm(q, g_q)
    cos_q, sin_q = rope_angles(pos, MLA_ROPE, MLA_THETA)
    q_nope = q[..., :MLA_NOPE] * g_k[:MLA_NOPE]
    q_rope = apply_rope(q[..., MLA_NOPE:], cos_q, sin_q)
    if past_ckv is None:
        ckv_all, kr_all = ckv_new, k_rope
    else:
        ckv_all = jnp.concatenate([past_ckv.astype(ckv_new.dtype), ckv_new], axis=1)
        kr_all = jnp.concatenate([past_krope.astype(k_rope.dtype), k_rope], axis=1)
    T = ckv_all.shape[1]
    k_nope = jnp.einsum('btr,rhd->bthd', ckv_all, w_uk)
    ms = (jnp.einsum('bthd,bthd->bth', k_nope, k_nope, preferred_element_type=jnp.float32)
          + jnp.einsum('btd,btd->bt', kr_all, kr_all, preferred_element_type=jnp.float32)[..., None]) / MLA_QK
    inv_k = jnp.transpose(lax.rsqrt(ms + EPS), (0, 2, 1))[:, :, None, :]
    cos_k, sin_k = rope_angles(jnp.arange(T), MLA_ROPE, MLA_THETA)
    k_rot = apply_rope(kr_all * g_k[MLA_NOPE:], cos_k, sin_k)
    k_pos = jnp.arange(T)

    def attend(args):
        qn, qr, qp = args
        s = (jnp.einsum('bqhd,bkhd->bhqk', qn, k_nope, preferred_element_type=jnp.float32)
             + jnp.einsum('bqhd,bkd->bhqk', qr, k_rot, preferred_element_type=jnp.float32))
        s = jnp.where(k_pos[None, None, None, :] <= qp[None, None, :, None], s * inv_k * MLA_SCALE, NEG)
        p = jax.nn.softmax(s, axis=-1)
        o_lat = jnp.einsum('bhqk,bkr->bqhr', p.astype(ckv_all.dtype), ckv_all)
        return jnp.einsum('bqhr,rhd->bqhd', o_lat, w_uv)

    mla_out = sweep_queries(attend, (q_nope, q_rope), pos, MLA_Q_BLOCK)
    mix = jnp.concatenate([pool_out, mla_out.reshape(B, S, MLA_HEADS * MLA_V).astype(pool_out.dtype)], axis=-1)
    return x + mix @ w_out, ckv_new, k_rope, pool_state


def odd_layer(x, pos, past_k, past_v, g_mix, w_qkv, g_q, g_k, w_o):
    B, S, _ = x.shape
    h = rms_norm(x, g_mix)
    q, k, v = jnp.split(h @ w_qkv, [C_HEADS * C_HEAD_DIM, (C_HEADS + C_KV_HEADS) * C_HEAD_DIM], axis=-1)
    cos, sin = rope_angles(pos, C_ROT, C_THETA)

    def partial_rope(t):
        return jnp.concatenate([apply_rope(t[..., :C_ROT], cos, sin), t[..., C_ROT:]], axis=-1)

    q = partial_rope(rms_norm(q.reshape(B, S, C_HEADS, C_HEAD_DIM), g_q))
    k = partial_rope(rms_norm(k.reshape(B, S, C_KV_HEADS, C_HEAD_DIM), g_k))
    v = v.reshape(B, S, C_KV_HEADS, C_HEAD_DIM)
    T = S if past_k is None else past_k.shape[1] + S
    nb = max(-(-T // MOBA_BLOCK), MOBA_TOPK)
    pad = jnp.zeros((B, nb * MOBA_BLOCK - T, C_KV_HEADS, C_HEAD_DIM), k.dtype)
    if past_k is None:
        k_all, v_all = jnp.concatenate([k, pad], axis=1), jnp.concatenate([v, pad], axis=1)
    else:
        k_all = jnp.concatenate([past_k.astype(k.dtype), k, pad], axis=1)
        v_all = jnp.concatenate([past_v.astype(v.dtype), v, pad], axis=1)
    k_blk = k_all.reshape(B, nb, MOBA_BLOCK, C_KV_HEADS, C_HEAD_DIM)
    v_blk = v_all.reshape(B, nb, MOBA_BLOCK, C_KV_HEADS, C_HEAD_DIM)
    k_mean = jnp.mean(k_blk.astype(jnp.float32), axis=2)
    bi = jnp.arange(B)[:, None, None, None, None]
    hi = jnp.arange(C_KV_HEADS)[None, None, :, None, None]

    def attend(args):
        qc, qp = args
        Q = qp.shape[0]
        qg = qc.reshape(B, Q, C_KV_HEADS, C_GROUP, C_HEAD_DIM)
        own = qp // MOBA_BLOCK
        gate = jnp.einsum('bqkgd,bnkd->bqkgn', qg.astype(jnp.float32), k_mean)
        fully_past = jnp.arange(nb)[None, :] < own[:, None]
        gate = jnp.where(fully_past[None, :, None, None, :], gate, NEG)
        _, sel = lax.top_k(gate, MOBA_TOPK)
        own_b = jnp.broadcast_to(own[None, :, None, None, None], sel.shape[:-1] + (1,)).astype(sel.dtype)
        idx = jnp.concatenate([sel, own_b], axis=-1)
        blk_ok = jnp.concatenate([sel < own_b, jnp.ones(own_b.shape, bool)], axis=-1)
        kg = k_blk[bi, idx, :, hi]
        vg = v_blk[bi, idx, :, hi]
        s = jnp.einsum('bqkgd,bqkgjsd->bqkgjs', qg, kg, preferred_element_type=jnp.float32) * C_SCALE
        kpos = idx[..., None] * MOBA_BLOCK + jnp.arange(MOBA_BLOCK)
        ok = blk_ok[..., None] & (kpos <= qp[None, :, None, None, None, None])
        s = jnp.where(ok, s, NEG)
        p = jax.nn.softmax(s.reshape(*s.shape[:4], -1), axis=-1).reshape(s.shape)
        o = jnp.einsum('bqkgjs,bqkgjsd->bqkgd', p.astype(vg.dtype), vg)
        return o.reshape(B, Q, C_HEADS, C_HEAD_DIM)

    o = sweep_queries(attend, (q,), pos, MOBA_Q_CHUNK)
    return x + o.reshape(B, S, C_MIX).astype(x.dtype) @ w_o, k, v


def swiglu_ffn(x, g, w_gate_up, w_down):
    gate, up = jnp.split(rms_norm(x, g) @ w_gate_up, 2, axis=-1)
    return x + (jax.nn.silu(gate) * up) @ w_down


def setup_inputs(seed: int = 0) -> dict:
    key = jax.random.key(seed)
    ks = list(jax.random.split(key, 32))
    f32 = jnp.float32

    def nrm(shape, scale=1.0):
        return scale * jax.random.normal(ks.pop(), shape, f32)

    def gain(shape):
        return 1.0 + 0.05 * jax.random.normal(ks.pop(), shape, f32)

    n_pages = PAST_LEN // PAGE_SIZE
    n_pool = (DEC_BATCH * n_pages * 5 + 3) // 4
    perm = jax.random.permutation(ks.pop(), n_pool)
    page_table = perm[:DEC_BATCH * n_pages].reshape(DEC_BATCH, n_pages).astype(jnp.int32)
    return {
        'x_prompt': nrm((BATCH, SEQ, D_MODEL)),
        'x_sample': nrm((DEC_BATCH, DEC_SEQ, D_MODEL)),
        'cache_mla_ckv': nrm((N_A, n_pool, PAGE_SIZE, MLA_KV_LORA)),
        'cache_mla_krope': nrm((N_A, n_pool, PAGE_SIZE, MLA_ROPE)),
        'state_pool': nrm((N_A, DEC_BATCH, POOL_STATE, POOL_WIDTH)),
        'cache_moba_k': nrm((N_C, n_pool, PAGE_SIZE, C_KV_HEADS, C_HEAD_DIM)),
        'cache_moba_v': nrm((N_C, n_pool, PAGE_SIZE, C_KV_HEADS, C_HEAD_DIM)),
        'page_table': page_table,
        'g_mix': gain((DEPTH, D_MODEL)),
        'g_ffn': gain((DEPTH, D_MODEL)),
        'w_in_a': nrm((N_A, D_MODEL, A_IN), D_MODEL ** -0.5),
        'g_q_lora': gain((N_A, MLA_Q_LORA)),
        'g_kv_lora': gain((N_A, MLA_KV_LORA)),
        'w_uq': nrm((N_A, MLA_Q_LORA, MLA_HEADS, MLA_QK), MLA_Q_LORA ** -0.5),
        'w_uk': nrm((N_A, MLA_KV_LORA, MLA_HEADS, MLA_NOPE), MLA_KV_LORA ** -0.5),
        'w_uv': nrm((N_A, MLA_KV_LORA, MLA_HEADS, MLA_V), MLA_KV_LORA ** -0.5),
        'g_mla_q': gain((N_A, MLA_QK)),
        'g_mla_k': gain((N_A, MLA_QK)),
        'w_pool': nrm((N_A, POOL_GROUPS, POOL_GROUP_DIM, POOL_GROUP_DIM), POOL_GROUP_DIM ** -0.5),
        's_pool': gain((N_A, POOL_WIDTH)),
        'w_out_a': nrm((N_A, A_MIX, D_MODEL), A_MIX ** -0.5),
        'w_qkv_c': nrm((N_C, D_MODEL, C_IN), D_MODEL ** -0.5),
        'g_moba_q': gain((N_C, C_HEAD_DIM)),
        'g_moba_k': gain((N_C, C_HEAD_DIM)),
        'w_o_c': nrm((N_C, C_MIX, D_MODEL), C_MIX ** -0.5),
        'w_gate_up': nrm((DEPTH, D_MODEL, 2 * FFN_HIDDEN), D_MODEL ** -0.5),
        'w_down': nrm((DEPTH, FFN_HIDDEN, D_MODEL), FFN_HIDDEN ** -0.5),
    }


def reference(x_prompt, x_sample, cache_mla_ckv, cache_mla_krope, state_pool, cache_moba_k, cache_moba_v,
              page_table, g_mix, g_ffn, w_in_a, g_q_lora, g_kv_lora, w_uq, w_uk, w_uv, g_mla_q, g_mla_k,
              w_pool, s_pool, w_out_a, w_qkv_c, g_moba_q, g_moba_k, w_o_c, w_gate_up, w_down):
    pos_p = jnp.arange(SEQ, dtype=jnp.int32)
    pos_s = PAST_LEN + jnp.arange(DEC_SEQ, dtype=jnp.int32)
    xp, xs = x_prompt, x_sample
    ckv_p, kr_p, pool_p, k_p, v_p = [], [], [], [], []
    ckv_s, kr_s, pool_s, k_s, v_s = [], [], [], [], []
    for layer in range(DEPTH):
        i = layer // 2
        if layer % 2 == 0:
            wa = (g_mix[layer], w_in_a[i], g_q_lora[i], g_kv_lora[i], w_uq[i], w_uk[i], w_uv[i],
                  g_mla_q[i], g_mla_k[i], w_pool[i], s_pool[i], w_out_a[i])
            xp, c1, c2, c3 = even_layer(xp, pos_p, None, None, None, *wa)
            xs, d1, d2, d3 = even_layer(xs, pos_s, gather_pages(cache_mla_ckv, i, page_table),
                                        gather_pages(cache_mla_krope, i, page_table), state_pool[i], *wa)
            ckv_p.append(c1); kr_p.append(c2); pool_p.append(c3)
            ckv_s.append(d1); kr_s.append(d2); pool_s.append(d3)
        else:
            wc = (g_mix[layer], w_qkv_c[i], g_moba_q[i], g_moba_k[i], w_o_c[i])
            xp, c1, c2 = odd_layer(xp, pos_p, None, None, *wc)
            xs, d1, d2 = odd_layer(xs, pos_s, gather_pages(cache_moba_k, i, page_table),
                                   gather_pages(cache_moba_v, i, page_table), *wc)
            k_p.append(c1); v_p.append(c2)
            k_s.append(d1); v_s.append(d2)
        xp = swiglu_ffn(xp, g_ffn[layer], w_gate_up[layer], w_down[layer])
        xs = swiglu_ffn(xs, g_ffn[layer], w_gate_up[layer], w_down[layer])
    return (xp, xs,
            jnp.stack(ckv_p), jnp.stack(kr_p), jnp.stack(pool_p), jnp.stack(k_p), jnp.stack(v_p),
            jnp.stack(ckv_s), jnp.stack(kr_s), jnp.stack(pool_s), jnp.stack(k_s), jnp.stack(v_s))
```

```python
import functools
import math

import jax
import jax.numpy as jnp
from jax import lax
from jax.experimental import pallas as pl
from jax.experimental.pallas import tpu as pltpu

F32 = jnp.float32
BF16 = jnp.bfloat16

D_MODEL = 1024
EPS = 1e-6
NEG = -1e30

POOL_WIDTH = 512
POOL_GROUPS = 4
POOL_GROUP_DIM = 128
POOL_WINDOWS = (2, 4, 8, 16)
POOL_STATE = 15
POOL_HALO = 16

MLA_HEADS = 8
MLA_NOPE = 64
MLA_ROPE = 32
MLA_QK = 96
MLA_V = 64
MLA_Q_LORA = 384
MLA_KV_LORA = 256
MLA_THETA = 10000.0
MLA_SCALE = MLA_QK ** -0.5
HEAD_PAD = 128
A_IN_PAD = POOL_WIDTH + MLA_Q_LORA + MLA_KV_LORA + HEAD_PAD

C_HEADS = 16
C_KV_HEADS = 4
C_GROUP = 4
C_HEAD_DIM = 64
C_ROT = 16
C_THETA = 500000.0
C_SCALE = C_HEAD_DIM ** -0.5
MOBA_BLOCK = 256
MOBA_TOPK = 3
C_Q_WIDTH = C_HEADS * C_HEAD_DIM
C_KV_WIDTH = C_KV_HEADS * C_HEAD_DIM

FFN_HIDDEN = 2816
FFN_CHUNK = 256

PAGE = 128
LANES = 128
VMEM_LIMIT = 56 * 1024 * 1024


def _cparams(*sem):
    return pltpu.CompilerParams(dimension_semantics=sem, vmem_limit_bytes=VMEM_LIMIT)


def _const_spec(shape):
    nd = len(shape)
    return pl.BlockSpec(shape, lambda *_: (0,) * nd, pipeline_mode=pl.Buffered(1))


def _rms(x, g):
    ms = jnp.mean(x * x, axis=-1, keepdims=True)
    return x * lax.rsqrt(ms + EPS) * g


def _dot(a, b):
    return jnp.dot(a, b, preferred_element_type=F32)


def _dot_nt(a, b):
    return lax.dot_general(a, b, (((1,), (1,)), ((), ())), preferred_element_type=F32)


def _prep_a_kernel(x_ref, gmix_ref, win_ref, gql_ref, gkvl_ref, wuq_ref, gq_ref, gkn_ref, gkr_ref,
                   wuk_ref, wuv_ref, cos_ref, sina_ref, sinb_ref, *rest, decode):
    if decode:
        wukt_ref, u_ref, ckv_ref, kr_ref, q_ref, k_ref, v_ref, qabs_ref = rest
    else:
        u_ref, ckv_ref, kr_ref, q_ref, k_ref, v_ref = rest
    h = _rms(x_ref[...], gmix_ref[...]).astype(BF16)
    hw = _dot(h, win_ref[...])
    u_ref[...] = hw[:, :POOL_WIDTH]
    o_q = POOL_WIDTH
    o_kv = o_q + MLA_Q_LORA
    o_kr = o_kv + MLA_KV_LORA
    cq = _rms(hw[:, o_q:o_kv], gql_ref[...]).astype(BF16)
    ckv = _rms(hw[:, o_kv:o_kr], gkvl_ref[...])
    ckv_ref[...] = ckv
    kr = hw[:, o_kr:o_kr + HEAD_PAD]
    kr_ref[...] = kr[:, MLA_NOPE:MLA_QK]
    cos = cos_ref[...]
    sina = sina_ref[...]
    sinb = sinb_ref[...]

    def rope(t, base):
        return t * base + pltpu.roll(t, HEAD_PAD - 16, 1) * sina + pltpu.roll(t, 16, 1) * sinb

    q = _dot(cq, wuq_ref[...])
    gq = gq_ref[...]
    qbase = gkn_ref[...] + cos
    for hd in range(MLA_HEADS):
        sl = slice(hd * HEAD_PAD, (hd + 1) * HEAD_PAD)
        blk = q[:, sl]
        ms = jnp.sum(blk * blk, axis=-1, keepdims=True) * (1.0 / MLA_QK)
        qt = (rope(blk * lax.rsqrt(ms + EPS) * gq, qbase) * MLA_SCALE).astype(BF16)
        q_ref[:, sl] = qt
        if decode:
            qabs_ref[:, hd * MLA_KV_LORA:(hd + 1) * MLA_KV_LORA] = _dot(qt, wukt_ref[hd]).astype(BF16)

    ckv_b = ckv.astype(BF16)
    kn = _dot(ckv_b, wuk_ref[...])
    krsq = jnp.sum(kr * kr, axis=-1, keepdims=True)
    krot = rope(kr * gkr_ref[...], cos)
    for hd in range(MLA_HEADS):
        sl = slice(hd * HEAD_PAD, (hd + 1) * HEAD_PAD)
        blk = kn[:, sl]
        ms = (jnp.sum(blk * blk, axis=-1, keepdims=True) + krsq) * (1.0 / MLA_QK)
        k_ref[:, sl] = ((blk + krot) * lax.rsqrt(ms + EPS)).astype(BF16)
    lane = lax.broadcasted_iota(jnp.int32, (1, MLA_HEADS * HEAD_PAD), 1)
    vv = _dot(ckv_b, wuv_ref[...])
    v_ref[...] = jnp.where(lane % HEAD_PAD < MLA_V, vv, 1.0).astype(BF16)


def _prep_a(x, gmix, wa, tabs, tm, decode):
    m = x.shape[0]
    n_tab = tabs[0].shape[0] // tm
    row = lambda i: (i, 0)
    tab = lambda i: (i % n_tab, 0)
    hp = MLA_HEADS * HEAD_PAD
    in_specs = [
        pl.BlockSpec((tm, D_MODEL), row), _const_spec((1, D_MODEL)), _const_spec((D_MODEL, A_IN_PAD)),
        _const_spec((1, MLA_Q_LORA)), _const_spec((1, MLA_KV_LORA)), _const_spec((MLA_Q_LORA, hp)),
        _const_spec((1, HEAD_PAD)), _const_spec((1, HEAD_PAD)), _const_spec((1, HEAD_PAD)),
        _const_spec((MLA_KV_LORA, hp)), _const_spec((MLA_KV_LORA, hp)),
        pl.BlockSpec((tm, HEAD_PAD), tab), pl.BlockSpec((tm, HEAD_PAD), tab), pl.BlockSpec((tm, HEAD_PAD), tab),
    ]
    args = [x, gmix, wa["w_in"], wa["g_q_lora"], wa["g_kv_lora"], wa["w_uq"], wa["g_q"], wa["g_kn"], wa["g_kr"],
            wa["w_uk"], wa["w_uv"], *tabs]
    out_shape = [
        jax.ShapeDtypeStruct((m, POOL_WIDTH), F32), jax.ShapeDtypeStruct((m, MLA_KV_LORA), F32),
        jax.ShapeDtypeStruct((m, MLA_ROPE), F32), jax.ShapeDtypeStruct((m, hp), BF16),
        jax.ShapeDtypeStruct((m, hp), BF16), jax.ShapeDtypeStruct((m, hp), BF16),
    ]
    out_specs = [
        pl.BlockSpec((tm, POOL_WIDTH), row), pl.BlockSpec((tm, MLA_KV_LORA), row),
        pl.BlockSpec((tm, MLA_ROPE), row), pl.BlockSpec((tm, hp), row),
        pl.BlockSpec((tm, hp), row), pl.BlockSpec((tm, hp), row),
    ]
    if decode:
        in_specs.append(_const_spec((MLA_HEADS, HEAD_PAD, MLA_KV_LORA)))
        args.append(wa["w_uk_t"])
        out_shape.append(jax.ShapeDtypeStruct((m, MLA_HEADS * MLA_KV_LORA), BF16))
        out_specs.append(pl.BlockSpec((tm, MLA_HEADS * MLA_KV_LORA), row))
    return pl.pallas_call(
        functools.partial(_prep_a_kernel, decode=decode),
        grid=(m // tm,), in_specs=in_specs, out_specs=out_specs, out_shape=out_shape,
        compiler_params=_cparams("parallel"), name="prep_a_dec" if decode else "prep_a",
    )(*args)


def _pool_mix(diffs, wp_ref, sp_ref, o_ref):
    for g in range(POOL_GROUPS):
        sl = slice(g * POOL_GROUP_DIM, (g + 1) * POOL_GROUP_DIM)
        o_ref[:, sl] = _dot(diffs[g].astype(BF16), wp_ref[g]) * sp_ref[:, sl]


def _pool_prompt_kernel(u_ref, halo_ref, wp_ref, sp_ref, o_ref, ext_ref):
    j = pl.program_id(1)
    tm = u_ref.shape[0]
    ext_ref[:POOL_HALO, :] = jnp.where(j > 0, halo_ref[...], 0.0)
    ext_ref[POOL_HALO:, :] = u_ref[...]
    pos = j * tm + lax.broadcasted_iota(jnp.int32, (tm, POOL_GROUP_DIM), 0)
    diffs = []
    for g, w in enumerate(POOL_WINDOWS):
        sl = slice(g * POOL_GROUP_DIM, (g + 1) * POOL_GROUP_DIM)
        cur = ext_ref[POOL_HALO:, sl]
        acc = cur
        for k in range(1, w):
            acc = acc + ext_ref[POOL_HALO - k:POOL_HALO - k + tm, sl]
        cnt = jnp.minimum(pos + 1, w).astype(F32)
        diffs.append(acc / cnt - cur)
    _pool_mix(diffs, wp_ref, sp_ref, o_ref)


def _pool_prompt(u, wp, sp, seq, tm):
    m = u.shape[0]
    nj = seq // tm
    hb = tm // POOL_HALO
    return pl.pallas_call(
        _pool_prompt_kernel,
        grid=(m // seq, nj),
        in_specs=[
            pl.BlockSpec((tm, POOL_WIDTH), lambda b, j: (b * nj + j, 0)),
            pl.BlockSpec((POOL_HALO, POOL_WIDTH), lambda b, j: (jnp.maximum((b * nj + j) * hb - 1, 0), 0)),
            _const_spec((POOL_GROUPS, POOL_GROUP_DIM, POOL_GROUP_DIM)), _const_spec((1, POOL_WIDTH)),
        ],
        out_specs=pl.BlockSpec((tm, POOL_WIDTH), lambda b, j: (b * nj + j, 0)),
        out_shape=jax.ShapeDtypeStruct((m, POOL_WIDTH), F32),
        scratch_shapes=[pltpu.VMEM((tm + POOL_HALO, POOL_WIDTH), F32)],
        compiler_params=_cparams("parallel", "parallel"), name="pool_prompt",
    )(u, u, wp, sp)


def _pool_sample_kernel(st_ref, u_ref, wp_ref, sp_ref, o_ref):
    diffs = []
    for g, w in enumerate(POOL_WINDOWS):
        sl = slice(g * POOL_GROUP_DIM, (g + 1) * POOL_GROUP_DIM)
        cur = u_ref[:, sl]
        acc = cur
        for k in range(1, w):
            acc = acc + st_ref[POOL_STATE - k, :, sl]
        diffs.append(acc / float(w) - cur)
    _pool_mix(diffs, wp_ref, sp_ref, o_ref)


def _pool_sample(state_t, u, wp, sp):
    m = u.shape[0]
    return pl.pallas_call(
        _pool_sample_kernel,
        grid=(1,),
        in_specs=[_const_spec(state_t.shape), _const_spec(u.shape),
                  _const_spec((POOL_GROUPS, POOL_GROUP_DIM, POOL_GROUP_DIM)), _const_spec((1, POOL_WIDTH))],
        out_specs=_const_spec((m, POOL_WIDTH)),
        out_shape=jax.ShapeDtypeStruct((m, POOL_WIDTH), F32),
        compiler_params=_cparams("arbitrary"), name="pool_sample",
    )(state_t, u, wp, sp)


def _mla_prompt_kernel(q_ref, k_ref, v_ref, o_ref):
    qi = pl.program_id(1)
    tq = q_ref.shape[0]
    rows = lax.broadcasted_iota(jnp.int32, (tq, tq), 0)
    cols = lax.broadcasted_iota(jnp.int32, (tq, tq), 1)
    for hd in range(MLA_HEADS):
        sl = slice(hd * HEAD_PAD, (hd + 1) * HEAD_PAD)
        q = q_ref[:, sl]

        def step(kb, carry, diag):
            m_prev, acc = carry
            start = pl.multiple_of(kb * tq, tq)
            s = _dot_nt(q, k_ref[pl.ds(start, tq), sl])
            if diag:
                s = jnp.where(cols <= rows, s, NEG)
            m_new = jnp.maximum(m_prev, jnp.max(s, axis=-1, keepdims=True))
            p = jnp.exp(s - m_new)
            acc = jnp.exp(m_prev - m_new) * acc + _dot(p.astype(BF16), v_ref[pl.ds(start, tq), sl])
            return m_new, acc

        init = (jnp.full((tq, 1), NEG, F32), jnp.zeros((tq, HEAD_PAD), F32))
        carry = lax.fori_loop(0, qi, functools.partial(step, diag=False), init)
        _, acc = step(qi, carry, True)
        o = acc / pltpu.roll(acc, MLA_V, 1)
        o_ref[:, hd * MLA_V:(hd + 1) * MLA_V] = o[:, :MLA_V]


def _mla_prompt(q, k, v, seq, tq):
    m = q.shape[0]
    nq = seq // tq
    hp = MLA_HEADS * HEAD_PAD
    return pl.pallas_call(
        _mla_prompt_kernel,
        grid=(m // seq, nq),
        in_specs=[
            pl.BlockSpec((tq, hp), lambda b, i: (b * nq + i, 0)),
            pl.BlockSpec((seq, hp), lambda b, i: (b, 0)),
            pl.BlockSpec((seq, hp), lambda b, i: (b, 0)),
        ],
        out_specs=pl.BlockSpec((tq, MLA_HEADS * MLA_V), lambda b, i: (b * nq + i, 0)),
        out_shape=jax.ShapeDtypeStruct((m, MLA_HEADS * MLA_V), F32),
        compiler_params=_cparams("parallel", "arbitrary"), name="mla_prompt",
    )(q, k, v)


def _page_copies(pt_ref, b, n_pages, srcs, bufs, sems, slot):
    out = []
    for j in range(n_pages):
        pg = pt_ref[b, j]
        for src, buf, sem in zip(srcs, bufs, sems):
            out.append(pltpu.make_async_copy(src.at[pg], buf.at[slot, pl.ds(j * PAGE, PAGE)], sem.at[slot]))
    return out


def _paged_prefetch(pt_ref, n_pages, srcs, bufs, sems):
    b = pl.program_id(0)
    nb = pl.num_programs(0)
    slot = b % 2

    @pl.when(b == 0)
    def _():
        for c in _page_copies(pt_ref, 0, n_pages, srcs, bufs, sems, 0):
            c.start()

    @pl.when(b + 1 < nb)
    def _():
        for c in _page_copies(pt_ref, b + 1, n_pages, srcs, bufs, sems, 1 - slot):
            c.start()

    for c in _page_copies(pt_ref, b, n_pages, srcs, bufs, sems, slot):
        c.wait()
    return slot


def _mla_decode_kernel(pt_ref, q_ref, qabs_ref, knew_ref, cnew_ref, gkr_ref, wukt_ref, cos_ref, sin_ref,
                       ckv_hbm, kr_hbm, o_ref, ckv_buf, kr_buf, s_scr, sem_c, sem_r, *, n_pages, chunk):
    t_past = n_pages * PAGE
    slot = _paged_prefetch(pt_ref, n_pages, (ckv_hbm, kr_hbm), (ckv_buf, kr_buf), (sem_c, sem_r))

    q = q_ref[0].astype(F32)
    s_new = jnp.sum(q * knew_ref[0].astype(F32), axis=-1, keepdims=True)
    lane = lax.broadcasted_iota(jnp.int32, (MLA_HEADS, HEAD_PAD), 1)
    first = (lane >= MLA_NOPE) & (lane < MLA_NOPE + 16)
    second = (lane >= MLA_NOPE + 16) & (lane < MLA_QK)
    psi2 = jnp.where(first, pltpu.roll(q, HEAD_PAD - 16, 1), 0.0) - jnp.where(second, pltpu.roll(q, 16, 1), 0.0)
    gkr = gkr_ref[...]
    psi1 = (q * gkr)[:, MLA_NOPE:MLA_QK].astype(BF16)
    psi2 = (psi2 * gkr)[:, MLA_NOPE:MLA_QK].astype(BF16)
    ones = jnp.ones((MLA_HEADS, MLA_ROPE), BF16)
    lhs = jnp.concatenate([wukt_ref[...], qabs_ref[0]], axis=0)
    n_up = MLA_HEADS * MLA_NOPE

    def score_chunk(c, _):
        start = pl.multiple_of(c * chunk, chunk)
        ckv_c = ckv_buf[slot, pl.ds(start, chunk), :].astype(BF16)
        r = _dot_nt(lhs, ckv_c)
        kn = r[:n_up]
        k2 = jnp.sum((kn * kn).reshape(MLA_HEADS, MLA_NOPE, chunk), axis=1)
        kr_c = kr_buf[slot, pl.ds(start, chunk), :]
        cos_c = cos_ref[pl.ds(start, chunk), :]
        sin_c = sin_ref[pl.ds(start, chunk), :]
        s_rope = _dot_nt(psi1, (kr_c * cos_c).astype(BF16)) + _dot_nt(psi2, (kr_c * sin_c).astype(BF16))
        kr2 = _dot_nt(ones, (kr_c * kr_c).astype(BF16))
        inv = lax.rsqrt((k2 + kr2) * (1.0 / MLA_QK) + EPS)
        s_scr[:, pl.ds(start, chunk)] = (r[n_up:n_up + MLA_HEADS] + s_rope) * inv
        return 0

    lax.fori_loop(0, t_past // chunk, score_chunk, 0)
    s = s_scr[...]
    m = jnp.maximum(jnp.max(s, axis=-1, keepdims=True), s_new)
    p = jnp.exp(s - m)
    p_new = jnp.exp(s_new - m)
    denom = jnp.sum(p, axis=-1, keepdims=True) + p_new
    s_scr[...] = p

    def pv_chunk(c, acc):
        start = pl.multiple_of(c * chunk, chunk)
        ckv_c = ckv_buf[slot, pl.ds(start, chunk), :].astype(BF16)
        return acc + _dot(s_scr[:, pl.ds(start, chunk)].astype(BF16), ckv_c)

    acc = lax.fori_loop(0, t_past // chunk, pv_chunk, jnp.zeros((MLA_HEADS, MLA_KV_LORA), F32))
    o_ref[0] = (acc + p_new * cnew_ref[0]) / denom


def _mla_decode(page_table, q, qabs16, knew, cnew, gkr, wukt2, cos32, sin32, cache_ckv, cache_kr, chunk):
    nb, n_pages = page_table.shape
    t_past = n_pages * PAGE
    kern = functools.partial(_mla_decode_kernel, n_pages=n_pages, chunk=chunk)
    b3 = lambda b, pt: (b, 0, 0)
    c2 = lambda b, pt: (0, 0)
    grid_spec = pltpu.PrefetchScalarGridSpec(
        num_scalar_prefetch=1, grid=(nb,),
        in_specs=[
            pl.BlockSpec((1, MLA_HEADS, HEAD_PAD), b3), pl.BlockSpec((1, 16, MLA_KV_LORA), b3),
            pl.BlockSpec((1, MLA_HEADS, HEAD_PAD), b3), pl.BlockSpec((1, 1, MLA_KV_LORA), b3),
            pl.BlockSpec((1, HEAD_PAD), c2), pl.BlockSpec((MLA_HEADS * MLA_NOPE, MLA_KV_LORA), c2),
            pl.BlockSpec((t_past, MLA_ROPE), c2), pl.BlockSpec((t_past, MLA_ROPE), c2),
            pl.BlockSpec(memory_space=pl.ANY), pl.BlockSpec(memory_space=pl.ANY),
        ],
        out_specs=pl.BlockSpec((1, MLA_HEADS, MLA_KV_LORA), b3),
        scratch_shapes=[
            pltpu.VMEM((2, t_past, MLA_KV_LORA), F32), pltpu.VMEM((2, t_past, MLA_ROPE), F32),
            pltpu.VMEM((MLA_HEADS, t_past), F32),
            pltpu.SemaphoreType.DMA((2,)), pltpu.SemaphoreType.DMA((2,)),
        ],
    )
    return pl.pallas_call(
        kern, grid_spec=grid_spec,
        out_shape=jax.ShapeDtypeStruct((nb, MLA_HEADS, MLA_KV_LORA), F32),
        compiler_params=_cparams("arbitrary"), name="mla_decode",
    )(page_table, q, qabs16, knew, cnew, gkr, wukt2, cos32, sin32, cache_ckv, cache_kr)


def _out_ffn_kernel(*refs, n_parts, latent):
    parts = refs[:n_parts]
    x_ref = refs[n_parts]
    idx = n_parts + 1
    if latent:
        wuv_ref = refs[idx]
        idx += 1
    wouts = refs[idx:idx + n_parts]
    gffn_ref, wgu_ref, wd_ref, o_ref, h_scr = refs[idx + n_parts:]
    mix = jnp.zeros(x_ref.shape, F32)
    for p_ref, w_ref in zip(parts, wouts):
        if latent and p_ref is parts[-1]:
            for hd in range(MLA_HEADS):
                lat = p_ref[:, hd * MLA_KV_LORA:(hd + 1) * MLA_KV_LORA].astype(BF16)
                a = _dot(lat, wuv_ref[hd]).astype(BF16)
                mix = mix + _dot(a, w_ref[hd * MLA_V:(hd + 1) * MLA_V, :])
        else:
            mix = mix + _dot(p_ref[...].astype(BF16), w_ref[...])
    x1 = x_ref[...] + mix
    h_scr[...] = _rms(x1, gffn_ref[...]).astype(BF16)
    acc = jnp.zeros_like(x1)
    for c in range(FFN_HIDDEN // FFN_CHUNK):
        lo = c * FFN_CHUNK
        gate = _dot(h_scr[...], wgu_ref[:, lo:lo + FFN_CHUNK])
        up = _dot(h_scr[...], wgu_ref[:, FFN_HIDDEN + lo:FFN_HIDDEN + lo + FFN_CHUNK])
        act = (gate / (1.0 + jnp.exp(-gate)) * up).astype(BF16)
        acc = acc + _dot(act, wd_ref[lo:lo + FFN_CHUNK, :])
    o_ref[...] = x1 + acc


def _out_ffn(parts, x, wouts, gffn, wgu, wd, tm, wuv=None):
    m = x.shape[0]
    row = lambda i: (i, 0)
    in_specs = [pl.BlockSpec((tm, p.shape[1]), row) for p in parts] + [pl.BlockSpec((tm, D_MODEL), row)]
    args = list(parts) + [x]
    if wuv is not None:
        in_specs.append(_const_spec(wuv.shape))
        args.append(wuv)
    in_specs += [_const_spec(w.shape) for w in wouts]
    in_specs += [_const_spec((1, D_MODEL)), _const_spec(wgu.shape), _const_spec(wd.shape)]
    args += list(wouts) + [gffn, wgu, wd]
    return pl.pallas_call(
        functools.partial(_out_ffn_kernel, n_parts=len(parts), latent=wuv is not None),
        grid=(m // tm,), in_specs=in_specs,
        out_specs=pl.BlockSpec((tm, D_MODEL), row),
        out_shape=jax.ShapeDtypeStruct((m, D_MODEL), F32),
        scratch_shapes=[pltpu.VMEM((tm, D_MODEL), BF16)],
        compiler_params=_cparams("parallel"), name="out_ffn",
    )(*args)


def _prep_c_kernel(x_ref, gmix_ref, wqkv_ref, gq_ref, gk_ref, seg_ref, ca_ref, cb1_ref, cb2_ref,
                   kf_ref, vf_ref, q_ref, k_ref, v_ref):
    h = _rms(x_ref[...], gmix_ref[...]).astype(BF16)
    qkv = _dot(h, wqkv_ref[...])
    ca = ca_ref[...]
    cb1 = cb1_ref[...]
    cb2 = cb2_ref[...]
    seg = seg_ref[...]

    def norm_rope(t, g):
        ms = _dot((t * t).astype(BF16), seg) * (1.0 / C_HEAD_DIM)
        t = t * lax.rsqrt(ms + EPS) * g
        return t * ca + pltpu.roll(t, LANES - C_ROT // 2, 1) * cb1 + pltpu.roll(t, C_ROT // 2, 1) * cb2

    lane = lax.broadcasted_iota(jnp.int32, (1, LANES), 1)
    low = lane < C_HEAD_DIM
    for j in range(C_Q_WIDTH // LANES):
        qt = norm_rope(qkv[:, j * LANES:(j + 1) * LANES], gq_ref[...])
        q_ref[j] = (qt * C_SCALE).astype(BF16)
    for j in range(C_KV_WIDTH // LANES):
        kt = norm_rope(qkv[:, C_Q_WIDTH + j * LANES:C_Q_WIDTH + (j + 1) * LANES], gk_ref[...])
        vt = qkv[:, C_Q_WIDTH + C_KV_WIDTH + j * LANES:C_Q_WIDTH + C_KV_WIDTH + (j + 1) * LANES]
        kf_ref[:, j * LANES:(j + 1) * LANES] = kt
        vf_ref[:, j * LANES:(j + 1) * LANES] = vt
        k_ref[2 * j] = jnp.where(low, kt, 0.0).astype(BF16)
        k_ref[2 * j + 1] = jnp.where(low, 0.0, kt).astype(BF16)
        v_ref[2 * j] = jnp.where(low, vt, 1.0).astype(BF16)
        v_ref[2 * j + 1] = jnp.where(low, 1.0, vt).astype(BF16)


def _prep_c(x, gmix, wc, tabs, tm):
    m = x.shape[0]
    n_tab = tabs[0].shape[0] // tm
    row = lambda i: (i, 0)
    tab = lambda i: (i % n_tab, 0)
    nq = C_Q_WIDTH // LANES
    return pl.pallas_call(
        _prep_c_kernel,
        grid=(m // tm,),
        in_specs=[
            pl.BlockSpec((tm, D_MODEL), row), _const_spec((1, D_MODEL)),
            _const_spec((D_MODEL, C_Q_WIDTH + 2 * C_KV_WIDTH)),
            _const_spec((1, LANES)), _const_spec((1, LANES)), _const_spec((LANES, LANES)),
            pl.BlockSpec((tm, LANES), tab), pl.BlockSpec((tm, LANES), tab), pl.BlockSpec((tm, LANES), tab),
        ],
        out_specs=[
            pl.BlockSpec((tm, C_KV_WIDTH), row), pl.BlockSpec((tm, C_KV_WIDTH), row),
            pl.BlockSpec((nq, tm, LANES), lambda i: (0, i, 0)),
            pl.BlockSpec((C_KV_HEADS, tm, LANES), lambda i: (0, i, 0)),
            pl.BlockSpec((C_KV_HEADS, tm, LANES), lambda i: (0, i, 0)),
        ],
        out_shape=[
            jax.ShapeDtypeStruct((m, C_KV_WIDTH), F32), jax.ShapeDtypeStruct((m, C_KV_WIDTH), F32),
            jax.ShapeDtypeStruct((nq, m, LANES), BF16),
            jax.ShapeDtypeStruct((C_KV_HEADS, m, LANES), BF16),
            jax.ShapeDtypeStruct((C_KV_HEADS, m, LANES), BF16),
        ],
        compiler_params=_cparams("parallel"), name="prep_c",
    )(x, gmix, wc["w_qkv"], wc["g_q"], wc["g_k"], wc["seg"], *tabs)


def _topk_rank_mask(g, valid, n_rows):
    row = lax.broadcasted_iota(jnp.int32, g.shape, 0)
    g = jnp.where(valid, g, NEG)
    cnt = jnp.zeros(g.shape, F32)
    for mrow in range(n_rows):
        gm = g[mrow:mrow + 1, :]
        cnt = cnt + jnp.where(row > mrow, jnp.where(gm >= g, 1.0, 0.0), jnp.where(gm > g, 1.0, 0.0))
    return jnp.where(valid & (cnt < MOBA_TOPK), 1.0, 0.0)


def _moba_prompt_kernel(q_ref, k_ref, v_ref, o_ref, m_scr, acc_scr, *, n_blk):
    qi = pl.program_id(1)
    blk = MOBA_BLOCK
    rows = C_GROUP * blk
    r_i = lax.broadcasted_iota(jnp.int32, (blk, blk), 0)
    c_i = lax.broadcasted_iota(jnp.int32, (blk, blk), 1)
    eye = jnp.where(r_i == c_i, 1.0, 0.0).astype(BF16)
    q_in_blk = lax.broadcasted_iota(jnp.int32, (rows, blk), 0) & (blk - 1)
    causal = lax.broadcasted_iota(jnp.int32, (rows, blk), 1) <= q_in_blk
    lane = lax.broadcasted_iota(jnp.int32, (1, LANES), 1)
    low = lane < C_HEAD_DIM
    cand = lax.broadcasted_iota(jnp.int32, (8, rows), 0)
    cand_k = lax.broadcasted_iota(jnp.int32, (8, LANES), 0)
    outs = []
    for kv in range(C_KV_HEADS):
        pair = kv // 2
        q = q_ref[pair * C_GROUP:(pair + 1) * C_GROUP].reshape(rows, LANES)
        kmean = jnp.zeros((8, LANES), F32)
        for n in range(n_blk):
            ksum = jnp.sum(k_ref[kv, n * blk:(n + 1) * blk, :].astype(F32), axis=0, keepdims=True)
            kmean = jnp.where(cand_k == n, ksum * (1.0 / blk), kmean)
        k_hi = kmean.astype(BF16)
        k_lo = (kmean - k_hi.astype(F32)).astype(BF16)
        gate_t = _dot_nt(k_hi, q) + _dot_nt(k_lo, q)
        sel_t = _topk_rank_mask(gate_t, cand < qi, 8).astype(BF16)
        sel = jnp.concatenate(
            [_dot_nt(eye, sel_t[:, g * blk:(g + 1) * blk]) for g in range(C_GROUP)], axis=0)

        m_scr[...] = jnp.full((rows, 1), NEG, F32)
        acc_scr[...] = jnp.zeros((rows, LANES), F32)

        def attend(s, v_blk):
            m_prev = m_scr[...]
            m_new = jnp.maximum(m_prev, jnp.max(s, axis=-1, keepdims=True))
            p = jnp.exp(s - m_new)
            acc_scr[...] = jnp.exp(m_prev - m_new) * acc_scr[...] + _dot(p.astype(BF16), v_blk)
            m_scr[...] = m_new

        for n in range(n_blk - 1):
            @pl.when(n < qi)
            def _():
                s = _dot_nt(q, k_ref[kv, n * blk:(n + 1) * blk, :])
                attend(s + (sel[:, n:n + 1] - 1.0) * (-NEG), v_ref[kv, n * blk:(n + 1) * blk, :])

        start = pl.multiple_of(qi * blk, blk)
        s = _dot_nt(q, k_ref[kv, pl.ds(start, blk), :])
        attend(jnp.where(causal, s, NEG), v_ref[kv, pl.ds(start, blk), :])
        acc = acc_scr[...]
        outs.append(acc / pltpu.roll(acc, C_HEAD_DIM, 1))
    for pair in range(C_KV_HEADS // 2):
        o = jnp.where(low, outs[2 * pair], outs[2 * pair + 1])
        o_ref[pair * C_GROUP:(pair + 1) * C_GROUP] = o.reshape(C_GROUP, blk, LANES).astype(BF16)


def _moba_prompt(q, k, v, seq):
    nq_t, m, _ = q.shape
    n_blk = seq // MOBA_BLOCK
    assert n_blk <= 8
    rows = C_GROUP * MOBA_BLOCK
    return pl.pallas_call(
        functools.partial(_moba_prompt_kernel, n_blk=n_blk),
        grid=(m // seq, n_blk),
        in_specs=[
            pl.BlockSpec((nq_t, MOBA_BLOCK, LANES), lambda b, i: (0, b * n_blk + i, 0)),
            pl.BlockSpec((C_KV_HEADS, seq, LANES), lambda b, i: (0, b, 0)),
            pl.BlockSpec((C_KV_HEADS, seq, LANES), lambda b, i: (0, b, 0)),
        ],
        out_specs=pl.BlockSpec((nq_t, MOBA_BLOCK, LANES), lambda b, i: (0, b * n_blk + i, 0)),
        out_shape=jax.ShapeDtypeStruct((nq_t, m, LANES), BF16),
        scratch_shapes=[pltpu.VMEM((rows, 1), F32), pltpu.VMEM((rows, LANES), F32)],
        compiler_params=_cparams("parallel", "arbitrary"), name="moba_prompt",
    )(q, k, v)


def _moba_decode_kernel(pt_ref, qbd_ref, knew_ref, vnew_ref, k_hbm, v_hbm, o_ref,
                        k_buf, v_buf, s_scr, sem_k, sem_v, *, n_pages, chunk):
    t_past = n_pages * PAGE
    n_blk = t_past // MOBA_BLOCK
    slot = _paged_prefetch(pt_ref, n_pages, (k_hbm, v_hbm), (k_buf, v_buf), (sem_k, sem_v))
    qbd = qbd_ref[0]
    s_new = jnp.sum(qbd.astype(F32) * knew_ref[0], axis=-1, keepdims=True)

    def score_chunk(c, _):
        start = pl.multiple_of(c * chunk, chunk)
        s_scr[:, pl.ds(start, chunk)] = _dot_nt(qbd, k_buf[slot, pl.ds(start, chunk), :].astype(BF16))
        return 0

    lax.fori_loop(0, t_past // chunk, score_chunk, 0)

    gates, maxes = [], []
    for n in range(n_blk):
        s_n = s_scr[:, n * MOBA_BLOCK:(n + 1) * MOBA_BLOCK]
        gates.append(jnp.sum(s_n, axis=-1, keepdims=True))
        maxes.append(jnp.max(s_n, axis=-1, keepdims=True))
    sels = []
    for n in range(n_blk):
        cnt = jnp.zeros_like(gates[n])
        for mb in range(n_blk):
            if mb != n:
                beats = (gates[mb] >= gates[n]) if mb < n else (gates[mb] > gates[n])
                cnt = cnt + jnp.where(beats, 1.0, 0.0)
        sels.append(cnt < MOBA_TOPK)
    m = s_new
    for n in range(n_blk):
        m = jnp.maximum(m, jnp.where(sels[n], maxes[n], NEG))
    p_new = jnp.exp(s_new - m)
    psum = jnp.zeros((C_HEADS, MOBA_BLOCK), F32)
    for n in range(n_blk):
        sl = slice(n * MOBA_BLOCK, (n + 1) * MOBA_BLOCK)
        p = jnp.where(sels[n], jnp.exp(s_scr[:, sl] - m), 0.0)
        s_scr[:, sl] = p
        psum = psum + p
    denom = jnp.sum(psum, axis=-1, keepdims=True) + p_new

    def pv_chunk(c, acc):
        start = pl.multiple_of(c * chunk, chunk)
        return acc + _dot(s_scr[:, pl.ds(start, chunk)].astype(BF16),
                          v_buf[slot, pl.ds(start, chunk), :].astype(BF16))

    acc = lax.fori_loop(0, t_past // chunk, pv_chunk, jnp.zeros((C_HEADS, C_KV_WIDTH), F32))
    o_ref[0] = (acc + p_new * vnew_ref[0]) / denom


def _moba_decode(page_table, qbd, knew, vnew, cache_k, cache_v, chunk):
    nb, n_pages = page_table.shape
    t_past = n_pages * PAGE
    b3 = lambda b, pt: (b, 0, 0)
    grid_spec = pltpu.PrefetchScalarGridSpec(
        num_scalar_prefetch=1, grid=(nb,),
        in_specs=[
            pl.BlockSpec((1, C_HEADS, C_KV_WIDTH), b3), pl.BlockSpec((1, 1, C_KV_WIDTH), b3),
            pl.BlockSpec((1, 1, C_KV_WIDTH), b3),
            pl.BlockSpec(memory_space=pl.ANY), pl.BlockSpec(memory_space=pl.ANY),
        ],
        out_specs=pl.BlockSpec((1, C_HEADS, C_KV_WIDTH), b3),
        scratch_shapes=[
            pltpu.VMEM((2, t_past, C_KV_WIDTH), F32), pltpu.VMEM((2, t_past, C_KV_WIDTH), F32),
            pltpu.VMEM((C_HEADS, t_past), F32),
            pltpu.SemaphoreType.DMA((2,)), pltpu.SemaphoreType.DMA((2,)),
        ],
    )
    return pl.pallas_call(
        functools.partial(_moba_decode_kernel, n_pages=n_pages, chunk=chunk), grid_spec=grid_spec,
        out_shape=jax.ShapeDtypeStruct((nb, C_HEADS, C_KV_WIDTH), F32),
        compiler_params=_cparams("arbitrary"), name="moba_decode",
    )(page_table, qbd, knew, vnew, cache_k, cache_v)


def _rope_angles(pos, dim, theta):
    inv_freq = jnp.exp(jnp.arange(0, dim, 2, dtype=F32) * (-math.log(theta) / dim))
    ang = pos.astype(F32)[:, None] * inv_freq[None, :]
    return jnp.cos(ang), jnp.sin(ang)


def _mla_tables(pos):
    cos, sin = _rope_angles(pos, MLA_ROPE, MLA_THETA)
    z = lambda n: jnp.zeros((pos.shape[0], n), F32)
    half = MLA_ROPE // 2
    tail = HEAD_PAD - MLA_QK
    cos_t = jnp.concatenate([z(MLA_NOPE), cos, cos, z(tail)], axis=1)
    sina = jnp.concatenate([z(MLA_NOPE), -sin, z(half), z(tail)], axis=1)
    sinb = jnp.concatenate([z(MLA_NOPE), z(half), sin, z(tail)], axis=1)
    return cos_t, sina, sinb


def _moba_tables(pos):
    cos, sin = _rope_angles(pos, C_ROT, C_THETA)
    n = pos.shape[0]
    half = C_ROT // 2
    rest = C_HEAD_DIM - C_ROT
    one = jnp.ones((n, rest), F32)
    z = lambda k: jnp.zeros((n, k), F32)
    ca = jnp.concatenate([cos, cos, one] * 2, axis=1)
    cb1 = jnp.concatenate([-sin, z(half), z(rest)] * 2, axis=1)
    cb2 = jnp.concatenate([z(half), sin, z(rest)] * 2, axis=1)
    return ca, cb1, cb2


def _pad_lanes(a, start, total):
    pad = [(0, 0)] * (a.ndim - 1) + [(start, total - start - a.shape[-1])]
    return jnp.pad(a, pad)


def _layer_a_weights(w_in, g_q_lora, g_kv_lora, w_uq, w_uk, w_uv, g_q, g_k, w_pool, s_pool, w_out):
    o_kr = POOL_WIDTH + MLA_Q_LORA + MLA_KV_LORA
    w_in_p = jnp.concatenate([w_in[:, :o_kr], _pad_lanes(w_in[:, o_kr:], MLA_NOPE, HEAD_PAD)], axis=1)
    hp = MLA_HEADS * HEAD_PAD
    return {
        "w_in": w_in_p.astype(BF16),
        "g_q_lora": g_q_lora[None], "g_kv_lora": g_kv_lora[None],
        "w_uq": _pad_lanes(w_uq, 0, HEAD_PAD).reshape(MLA_Q_LORA, hp).astype(BF16),
        "w_uk": _pad_lanes(w_uk, 0, HEAD_PAD).reshape(MLA_KV_LORA, hp).astype(BF16),
        "w_uv": _pad_lanes(w_uv, 0, HEAD_PAD).reshape(MLA_KV_LORA, hp).astype(BF16),
        "w_uk_t": _pad_lanes(jnp.transpose(w_uk, (1, 0, 2)), 0, HEAD_PAD).transpose(0, 2, 1).astype(BF16),
        "w_uk_t2": jnp.transpose(w_uk, (1, 2, 0)).reshape(MLA_HEADS * MLA_NOPE, MLA_KV_LORA).astype(BF16),
        "w_uv_h": jnp.transpose(w_uv, (1, 0, 2)).astype(BF16),
        "g_q": _pad_lanes(g_q, 0, HEAD_PAD)[None],
        "g_kn": _pad_lanes(g_k[:MLA_NOPE], 0, HEAD_PAD)[None],
        "g_kr": _pad_lanes(g_k[MLA_NOPE:], MLA_NOPE, HEAD_PAD)[None],
        "w_pool": w_pool.astype(BF16), "s_pool": s_pool[None],
        "w_out_pool": w_out[:POOL_WIDTH].astype(BF16), "w_out_mla": w_out[POOL_WIDTH:].astype(BF16),
    }


def _moba_tile_heads():
    order = []
    for j in range(C_Q_WIDTH // LANES):
        p, g = divmod(j, C_GROUP)
        order += [(2 * p) * C_GROUP + g, (2 * p + 1) * C_GROUP + g]
    return order


def _layer_c_weights(w_qkv, g_q, g_k, w_o):
    order = jnp.array(_moba_tile_heads())
    wq = w_qkv[:, :C_Q_WIDTH].reshape(D_MODEL, C_HEADS, C_HEAD_DIM)[:, order].reshape(D_MODEL, C_Q_WIDTH)
    w_o_p = w_o.reshape(C_HEADS, C_HEAD_DIM, D_MODEL)[order].reshape(C_Q_WIDTH, D_MODEL)
    seg = jnp.kron(jnp.eye(LANES // C_HEAD_DIM, dtype=F32), jnp.ones((C_HEAD_DIM, C_HEAD_DIM), F32))
    return {
        "w_qkv": jnp.concatenate([wq, w_qkv[:, C_Q_WIDTH:]], axis=1).astype(BF16),
        "g_q": jnp.tile(g_q, 2)[None], "g_k": jnp.tile(g_k, 2)[None], "seg": seg.astype(BF16),
        "w_o": w_o_p.astype(BF16),
    }


def _tiles_to_rows(t):
    return jnp.transpose(t, (1, 0, 2)).reshape(t.shape[1], t.shape[0] * LANES)


def _pick_tile(m, pref):
    return pref if m % pref == 0 else m


def kernel(x_prompt, x_sample, cache_mla_ckv, cache_mla_krope, state_pool, cache_moba_k, cache_moba_v,
           page_table, g_mix, g_ffn, w_in_a, g_q_lora, g_kv_lora, w_uq, w_uk, w_uv, g_mla_q, g_mla_k,
           w_pool, s_pool, w_out_a, w_qkv_c, g_moba_q, g_moba_k, w_o_c, w_gate_up, w_down):
    bp, seq, _ = x_prompt.shape
    bs = x_sample.shape[0]
    n_pages = page_table.shape[1]
    past = n_pages * PAGE
    depth = g_mix.shape[0]
    mp = bp * seq
    xp = x_prompt.reshape(mp, D_MODEL)
    xs = x_sample.reshape(bs, D_MODEL)
    tm_p = _pick_tile(mp, 256)
    tm_f = _pick_tile(mp, 512)
    tq = 256
    chunk = min(1024, past)

    pos_p = jnp.arange(seq, dtype=jnp.int32)
    pos_s = jnp.full((bs,), past, dtype=jnp.int32)
    wgu = w_gate_up.astype(BF16)
    wdn = w_down.astype(BF16)

    outs_p = {k: [] for k in ("ckv", "kr", "pool", "k", "v")}
    outs_s = {k: [] for k in ("ckv", "kr", "pool", "k", "v")}
    for layer in range(depth):
        i = layer // 2
        gmix = g_mix[layer][None]
        gffn = g_ffn[layer][None]
        if layer % 2 == 0:
            wa = _layer_a_weights(w_in_a[i], g_q_lora[i], g_kv_lora[i], w_uq[i], w_uk[i], w_uv[i],
                                  g_mla_q[i], g_mla_k[i], w_pool[i], s_pool[i], w_out_a[i])
            u, ckv, kr, q, k, v = _prep_a(xp, gmix, wa, _mla_tables(pos_p), tm_p, decode=False)
            pool_out = _pool_prompt(u, wa["w_pool"], wa["s_pool"], seq, tm_p)
            mla_out = _mla_prompt(q, k, v, seq, tq)
            xp = _out_ffn([pool_out, mla_out], xp, [wa["w_out_pool"], wa["w_out_mla"]], gffn,
                          wgu[layer], wdn[layer], tm_f)
            outs_p["ckv"].append(ckv.reshape(bp, seq, MLA_KV_LORA))
            outs_p["kr"].append(kr.reshape(bp, seq, MLA_ROPE))
            outs_p["pool"].append(u.reshape(bp, seq, POOL_WIDTH)[:, seq - POOL_STATE:])
            u, ckv, kr, q, k, v, qabs = _prep_a(xs, gmix, wa, _mla_tables(pos_s), bs, decode=True)
            state = state_pool[i]
            pool_out = _pool_sample(jnp.transpose(state, (1, 0, 2)), u, wa["w_pool"], wa["s_pool"])
            cos_k, sin_k = _rope_angles(jnp.arange(past, dtype=jnp.int32), MLA_ROPE, MLA_THETA)
            qabs16 = jnp.pad(qabs.reshape(bs, MLA_HEADS, MLA_KV_LORA), ((0, 0), (0, 16 - MLA_HEADS), (0, 0)))
            o_lat = _mla_decode(
                page_table, q.reshape(bs, MLA_HEADS, HEAD_PAD), qabs16, k.reshape(bs, MLA_HEADS, HEAD_PAD),
                ckv.reshape(bs, 1, MLA_KV_LORA), wa["g_kr"], wa["w_uk_t2"],
                jnp.concatenate([cos_k, cos_k], axis=1), jnp.concatenate([sin_k, sin_k], axis=1),
                cache_mla_ckv[i], cache_mla_krope[i], chunk)
            xs = _out_ffn([pool_out, o_lat.reshape(bs, MLA_HEADS * MLA_KV_LORA)], xs,
                          [wa["w_out_pool"], wa["w_out_mla"]], gffn, wgu[layer], wdn[layer], bs,
                          wuv=wa["w_uv_h"])
            outs_s["ckv"].append(ckv.reshape(bs, 1, MLA_KV_LORA))
            outs_s["kr"].append(kr.reshape(bs, 1, MLA_ROPE))
            outs_s["pool"].append(jnp.concatenate([state[:, 1:], u[:, None]], axis=1))
        else:
            wc = _layer_c_weights(w_qkv_c[i], g_moba_q[i], g_moba_k[i], w_o_c[i])
            kf, vf, q, k, v = _prep_c(xp, gmix, wc, _moba_tables(pos_p), tm_p)
            o = _moba_prompt(q, k, v, seq)
            xp = _out_ffn([_tiles_to_rows(o)], xp, [wc["w_o"]], gffn, wgu[layer], wdn[layer], tm_f)
            outs_p["k"].append(kf.reshape(bp, seq, C_KV_HEADS, C_HEAD_DIM))
            outs_p["v"].append(vf.reshape(bp, seq, C_KV_HEADS, C_HEAD_DIM))
            kf, vf, q, k, v = _prep_c(xs, gmix, wc, _moba_tables(pos_s), bs)
            qh = _tiles_to_rows(q).reshape(bs, C_Q_WIDTH // LANES, 2, C_HEAD_DIM)
            kv_of = jnp.array(_moba_tile_heads()).reshape(-1, 2) // C_GROUP
            onehot = (kv_of[:, :, None] == jnp.arange(C_KV_HEADS)[None, None, :]).astype(BF16)
            qbd = (qh[:, :, :, None, :] * onehot[None, :, :, :, None]).reshape(bs, C_HEADS, C_KV_WIDTH)
            o = _moba_decode(page_table, qbd, kf.reshape(bs, 1, C_KV_WIDTH), vf.reshape(bs, 1, C_KV_WIDTH),
                             cache_moba_k[i].reshape(-1, PAGE, C_KV_WIDTH),
                             cache_moba_v[i].reshape(-1, PAGE, C_KV_WIDTH), chunk)
            o = (o.reshape(bs, C_Q_WIDTH // LANES, 2, C_KV_HEADS, C_HEAD_DIM)
                 * onehot[None, :, :, :, None].astype(F32)).sum(axis=3).reshape(bs, C_Q_WIDTH)
            xs = _out_ffn([o], xs, [wc["w_o"]], gffn, wgu[layer], wdn[layer], bs)
            outs_s["k"].append(kf.reshape(bs, 1, C_KV_HEADS, C_HEAD_DIM))
            outs_s["v"].append(vf.reshape(bs, 1, C_KV_HEADS, C_HEAD_DIM))

    st = lambda xs_: jnp.stack(xs_)
    return (xp.reshape(bp, seq, D_MODEL), xs.reshape(bs, 1, D_MODEL),
            st(outs_p["ckv"]), st(outs_p["kr"]), st(outs_p["pool"]), st(outs_p["k"]), st(outs_p["v"]),
            st(outs_s["ckv"]), st(outs_s["kr"]), st(outs_s["pool"]), st(outs_s["k"]), st(outs_s["v"]))
```

```python
import functools
import math

import jax
import jax.numpy as jnp
from jax import lax
from jax.experimental import pallas as pl
from jax.experimental.pallas import tpu as pltpu

F32 = jnp.float32
BF16 = jnp.bfloat16

D_MODEL = 1024
EPS = 1e-6
NEG = -1e30

POOL_WIDTH = 512
POOL_GROUPS = 4
POOL_GROUP_DIM = 128
POOL_WINDOWS = (2, 4, 8, 16)
POOL_STATE = 15
POOL_HALO = 16

MLA_HEADS = 8
MLA_NOPE = 64
MLA_ROPE = 32
MLA_QK = 96
MLA_V = 64
MLA_Q_LORA = 384
MLA_KV_LORA = 256
MLA_THETA = 10000.0
MLA_SCALE = MLA_QK ** -0.5
HEAD_PAD = 128
A_IN_PAD = POOL_WIDTH + MLA_Q_LORA + MLA_KV_LORA + HEAD_PAD

C_HEADS = 16
C_KV_HEADS = 4
C_GROUP = 4
C_HEAD_DIM = 64
C_ROT = 16
C_THETA = 500000.0
C_SCALE = C_HEAD_DIM ** -0.5
MOBA_BLOCK = 256
MOBA_TOPK = 3
C_Q_WIDTH = C_HEADS * C_HEAD_DIM
C_KV_WIDTH = C_KV_HEADS * C_HEAD_DIM

FFN_HIDDEN = 2816
FFN_CHUNK = 256

PAGE = 128
LANES = 128
VMEM_LIMIT = 56 * 1024 * 1024


def _cparams(*sem):
    return pltpu.CompilerParams(dimension_semantics=sem, vmem_limit_bytes=VMEM_LIMIT)


def _const_spec(shape):
    nd = len(shape)
    return pl.BlockSpec(shape, lambda *_: (0,) * nd, pipeline_mode=pl.Buffered(1))


def _rms(x, g):
    ms = jnp.mean(x * x, axis=-1, keepdims=True)
    return x * lax.rsqrt(ms + EPS) * g


def _dot(a, b):
    return jnp.dot(a, b, preferred_element_type=F32)


def _dot_nt(a, b):
    return lax.dot_general(a, b, (((1,), (1,)), ((), ())), preferred_element_type=F32)


def _prep_a_kernel(x_ref, gmix_ref, win_ref, gql_ref, gkvl_ref, wuq_ref, gq_ref, gkn_ref, gkr_ref,
                   wuk_ref, wuv_ref, cos_ref, sina_ref, sinb_ref, *rest, decode):
    if decode:
        wukt_ref, u_ref, ckv_ref, kr_ref, q_ref, k_ref, v_ref, qabs_ref = rest
    else:
        u_ref, ckv_ref, kr_ref, q_ref, k_ref, v_ref = rest
    h = _rms(x_ref[...], gmix_ref[...]).astype(BF16)
    hw = _dot(h, win_ref[...])
    u_ref[...] = hw[:, :POOL_WIDTH]
    o_q = POOL_WIDTH
    o_kv = o_q + MLA_Q_LORA
    o_kr = o_kv + MLA_KV_LORA
    cq = _rms(hw[:, o_q:o_kv], gql_ref[...]).astype(BF16)
    ckv = _rms(hw[:, o_kv:o_kr], gkvl_ref[...])
    ckv_ref[...] = ckv
    kr = hw[:, o_kr:o_kr + HEAD_PAD]
    kr_ref[...] = kr[:, MLA_NOPE:MLA_QK]
    cos = cos_ref[...]
    sina = sina_ref[...]
    sinb = sinb_ref[...]

    def rope(t, base):
        return t * base + pltpu.roll(t, HEAD_PAD - 16, 1) * sina + pltpu.roll(t, 16, 1) * sinb

    q = _dot(cq, wuq_ref[...])
    gq = gq_ref[...]
    qbase = gkn_ref[...] + cos
    for hd in range(MLA_HEADS):
        sl = slice(hd * HEAD_PAD, (hd + 1) * HEAD_PAD)
        blk = q[:, sl]
        ms = jnp.sum(blk * blk, axis=-1, keepdims=True) * (1.0 / MLA_QK)
        qt = (rope(blk * lax.rsqrt(ms + EPS) * gq, qbase) * MLA_SCALE).astype(BF16)
        q_ref[:, sl] = qt
        if decode:
            qabs_ref[:, hd * MLA_KV_LORA:(hd + 1) * MLA_KV_LORA] = _dot(qt, wukt_ref[hd]).astype(BF16)

    ckv_b = ckv.astype(BF16)
    kn = _dot(ckv_b, wuk_ref[...])
    krsq = jnp.sum(kr * kr, axis=-1, keepdims=True)
    krot = rope(kr * gkr_ref[...], cos)
    for hd in range(MLA_HEADS):
        sl = slice(hd * HEAD_PAD, (hd + 1) * HEAD_PAD)
        blk = kn[:, sl]
        ms = (jnp.sum(blk * blk, axis=-1, keepdims=True) + krsq) * (1.0 / MLA_QK)
        k_ref[:, sl] = ((blk + krot) * lax.rsqrt(ms + EPS)).astype(BF16)
    lane = lax.broadcasted_iota(jnp.int32, (1, MLA_HEADS * HEAD_PAD), 1)
    vv = _dot(ckv_b, wuv_ref[...])
    v_ref[...] = jnp.where(lane % HEAD_PAD < MLA_V, vv, 1.0).astype(BF16)


def _prep_a(x, gmix, wa, tabs, tm, decode):
    m = x.shape[0]
    n_tab = tabs[0].shape[0] // tm
    row = lambda i: (i, 0)
    tab = lambda i: (i % n_tab, 0)
    hp = MLA_HEADS * HEAD_PAD
    in_specs = [
        pl.BlockSpec((tm, D_MODEL), row), _const_spec((1, D_MODEL)), _const_spec((D_MODEL, A_IN_PAD)),
        _const_spec((1, MLA_Q_LORA)), _const_spec((1, MLA_KV_LORA)), _const_spec((MLA_Q_LORA, hp)),
        _const_spec((1, HEAD_PAD)), _const_spec((1, HEAD_PAD)), _const_spec((1, HEAD_PAD)),
        _const_spec((MLA_KV_LORA, hp)), _const_spec((MLA_KV_LORA, hp)),
        pl.BlockSpec((tm, HEAD_PAD), tab), pl.BlockSpec((tm, HEAD_PAD), tab), pl.BlockSpec((tm, HEAD_PAD), tab),
    ]
    args = [x, gmix, wa["w_in"], wa["g_q_lora"], wa["g_kv_lora"], wa["w_uq"], wa["g_q"], wa["g_kn"], wa["g_kr"],
            wa["w_uk"], wa["w_uv"], *tabs]
    out_shape = [
        jax.ShapeDtypeStruct((m, POOL_WIDTH), F32), jax.ShapeDtypeStruct((m, MLA_KV_LORA), F32),
        jax.ShapeDtypeStruct((m, MLA_ROPE), F32), jax.ShapeDtypeStruct((m, hp), BF16),
        jax.ShapeDtypeStruct((m, hp), BF16), jax.ShapeDtypeStruct((m, hp), BF16),
    ]
    out_specs = [
        pl.BlockSpec((tm, POOL_WIDTH), row), pl.BlockSpec((tm, MLA_KV_LORA), row),
        pl.BlockSpec((tm, MLA_ROPE), row), pl.BlockSpec((tm, hp), row),
        pl.BlockSpec((tm, hp), row), pl.BlockSpec((tm, hp), row),
    ]
    if decode:
        in_specs.append(_const_spec((MLA_HEADS, HEAD_PAD, MLA_KV_LORA)))
        args.append(wa["w_uk_t"])
        out_shape.append(jax.ShapeDtypeStruct((m, MLA_HEADS * MLA_KV_LORA), BF16))
        out_specs.append(pl.BlockSpec((tm, MLA_HEADS * MLA_KV_LORA), row))
    return pl.pallas_call(
        functools.partial(_prep_a_kernel, decode=decode),
        grid=(m // tm,), in_specs=in_specs, out_specs=out_specs, out_shape=out_shape,
        compiler_params=_cparams("parallel"), name="prep_a_dec" if decode else "prep_a",
    )(*args)


def _pool_mix(diffs, wp_ref, sp_ref, o_ref):
    for g in range(POOL_GROUPS):
        sl = slice(g * POOL_GROUP_DIM, (g + 1) * POOL_GROUP_DIM)
        o_ref[:, sl] = _dot(diffs[g].astype(BF16), wp_ref[g]) * sp_ref[:, sl]


def _pool_prompt_kernel(u_ref, halo_ref, wp_ref, sp_ref, o_ref, ext_ref):
    j = pl.program_id(1)
    tm = u_ref.shape[0]
    ext_ref[:POOL_HALO, :] = jnp.where(j > 0, halo_ref[...], 0.0)
    ext_ref[POOL_HALO:, :] = u_ref[...]
    pos = j * tm + lax.broadcasted_iota(jnp.int32, (tm, POOL_GROUP_DIM), 0)
    diffs = []
    for g, w in enumerate(POOL_WINDOWS):
        sl = slice(g * POOL_GROUP_DIM, (g + 1) * POOL_GROUP_DIM)
        cur = ext_ref[POOL_HALO:, sl]
        acc = cur
        for k in range(1, w):
            acc = acc + ext_ref[POOL_HALO - k:POOL_HALO - k + tm, sl]
        cnt = jnp.minimum(pos + 1, w).astype(F32)
        diffs.append(acc / cnt - cur)
    _pool_mix(diffs, wp_ref, sp_ref, o_ref)


def _pool_prompt(u, wp, sp, seq, tm):
    m = u.shape[0]
    nj = seq // tm
    hb = tm // POOL_HALO
    return pl.pallas_call(
        _pool_prompt_kernel,
        grid=(m // seq, nj),
        in_specs=[
            pl.BlockSpec((tm, POOL_WIDTH), lambda b, j: (b * nj + j, 0)),
            pl.BlockSpec((POOL_HALO, POOL_WIDTH), lambda b, j: (jnp.maximum((b * nj + j) * hb - 1, 0), 0)),
            _const_spec((POOL_GROUPS, POOL_GROUP_DIM, POOL_GROUP_DIM)), _const_spec((1, POOL_WIDTH)),
        ],
        out_specs=pl.BlockSpec((tm, POOL_WIDTH), lambda b, j: (b * nj + j, 0)),
        out_shape=jax.ShapeDtypeStruct((m, POOL_WIDTH), F32),
        scratch_shapes=[pltpu.VMEM((tm + POOL_HALO, POOL_WIDTH), F32)],
        compiler_params=_cparams("parallel", "parallel"), name="pool_prompt",
    )(u, u, wp, sp)


def _pool_sample_kernel(st_ref, u_ref, wp_ref, sp_ref, o_ref):
    diffs = []
    for g, w in enumerate(POOL_WINDOWS):
        sl = slice(g * POOL_GROUP_DIM, (g + 1) * POOL_GROUP_DIM)
        cur = u_ref[:, sl]
        acc = cur
        for k in range(1, w):
            acc = acc + st_ref[POOL_STATE - k, :, sl]
        diffs.append(acc / float(w) - cur)
    _pool_mix(diffs, wp_ref, sp_ref, o_ref)


def _pool_sample(state_t, u, wp, sp):
    m = u.shape[0]
    return pl.pallas_call(
        _pool_sample_kernel,
        grid=(1,),
        in_specs=[_const_spec(state_t.shape), _const_spec(u.shape),
                  _const_spec((POOL_GROUPS, POOL_GROUP_DIM, POOL_GROUP_DIM)), _const_spec((1, POOL_WIDTH))],
        out_specs=_const_spec((m, POOL_WIDTH)),
        out_shape=jax.ShapeDtypeStruct((m, POOL_WIDTH), F32),
        compiler_params=_cparams("arbitrary"), name="pool_sample",
    )(state_t, u, wp, sp)


def _mla_prompt_kernel(q_ref, k_ref, v_ref, o_ref):
    qi = pl.program_id(1)
    tq = q_ref.shape[0]
    rows = lax.broadcasted_iota(jnp.int32, (tq, tq), 0)
    cols = lax.broadcasted_iota(jnp.int32, (tq, tq), 1)
    for hd in range(MLA_HEADS):
        sl = slice(hd * HEAD_PAD, (hd + 1) * HEAD_PAD)
        q = q_ref[:, sl]

        def step(kb, carry, diag):
            m_prev, acc = carry
            start = pl.multiple_of(kb * tq, tq)
            s = _dot_nt(q, k_ref[pl.ds(start, tq), sl])
            if diag:
                s = jnp.where(cols <= rows, s, NEG)
            m_new = jnp.maximum(m_prev, jnp.max(s, axis=-1, keepdims=True))
            p = jnp.exp(s - m_new)
            acc = jnp.exp(m_prev - m_new) * acc + _dot(p.astype(BF16), v_ref[pl.ds(start, tq), sl])
            return m_new, acc

        init = (jnp.full((tq, 1), NEG, F32), jnp.zeros((tq, HEAD_PAD), F32))
        carry = lax.fori_loop(0, qi, functools.partial(step, diag=False), init)
        _, acc = step(qi, carry, True)
        o = acc / pltpu.roll(acc, MLA_V, 1)
        o_ref[:, hd * MLA_V:(hd + 1) * MLA_V] = o[:, :MLA_V]


def _mla_prompt(q, k, v, seq, tq):
    m = q.shape[0]
    nq = seq // tq
    hp = MLA_HEADS * HEAD_PAD
    return pl.pallas_call(
        _mla_prompt_kernel,
        grid=(m // seq, nq),
        in_specs=[
            pl.BlockSpec((tq, hp), lambda b, i: (b * nq + i, 0)),
            pl.BlockSpec((seq, hp), lambda b, i: (b, 0)),
            pl.BlockSpec((seq, hp), lambda b, i: (b, 0)),
        ],
        out_specs=pl.BlockSpec((tq, MLA_HEADS * MLA_V), lambda b, i: (b * nq + i, 0)),
        out_shape=jax.ShapeDtypeStruct((m, MLA_HEADS * MLA_V), F32),
        compiler_params=_cparams("parallel", "arbitrary"), name="mla_prompt",
    )(q, k, v)


def _page_copies(pt_ref, b, n_pages, srcs, bufs, sems, token_axes, slot):
    out = []
    for j in range(n_pages):
        pg = pt_ref[b, j]
        for src, buf, sem, axis in zip(srcs, bufs, sems, token_axes):
            window = pl.ds(j * PAGE, PAGE)
            dst = buf.at[slot, window] if axis == 0 else buf.at[slot, :, window]
            out.append(pltpu.make_async_copy(src.at[pg], dst, sem.at[slot]))
    return out


def _paged_prefetch(pt_ref, n_pages, srcs, bufs, sems, token_axes):
    b = pl.program_id(0)
    nb = pl.num_programs(0)
    slot = b % 2
    copies = functools.partial(_page_copies, pt_ref, n_pages=n_pages, srcs=srcs, bufs=bufs, sems=sems,
                               token_axes=token_axes)

    @pl.when(b == 0)
    def _():
        for c in copies(b=0, slot=0):
            c.start()

    @pl.when(b + 1 < nb)
    def _():
        for c in copies(b=b + 1, slot=1 - slot):
            c.start()

    for c in copies(b=b, slot=slot):
        c.wait()
    return slot


def _mla_decode_kernel(pt_ref, q_ref, qabs_ref, knew_ref, cnew_ref, gkr_ref, wukt_ref, cos_ref, sin_ref,
                       ckv_hbm, kr_hbm, o_ref, ckv_buf, kr_buf, s_scr, sem_c, sem_r, *, n_pages, chunk):
    t_past = n_pages * PAGE
    slot = _paged_prefetch(pt_ref, n_pages, (ckv_hbm, kr_hbm), (ckv_buf, kr_buf), (sem_c, sem_r), (0, 1))

    q = q_ref[0].astype(F32)
    s_new = jnp.sum(q * knew_ref[0].astype(F32), axis=-1, keepdims=True)
    lane = lax.broadcasted_iota(jnp.int32, (MLA_HEADS, HEAD_PAD), 1)
    first = (lane >= MLA_NOPE) & (lane < MLA_NOPE + 16)
    second = (lane >= MLA_NOPE + 16) & (lane < MLA_QK)
    psi2 = jnp.where(first, pltpu.roll(q, HEAD_PAD - 16, 1), 0.0) - jnp.where(second, pltpu.roll(q, 16, 1), 0.0)
    gkr = gkr_ref[...]
    psi1 = (q * gkr)[:, MLA_NOPE:MLA_QK].astype(BF16)
    psi2 = (psi2 * gkr)[:, MLA_NOPE:MLA_QK].astype(BF16)
    lhs = jnp.concatenate([wukt_ref[...], qabs_ref[0]], axis=0)
    n_up = MLA_HEADS * MLA_NOPE

    def score_chunk(c, _):
        start = pl.multiple_of(c * chunk, chunk)
        ckv_c = ckv_buf[slot, pl.ds(start, chunk), :].astype(BF16)
        r = _dot_nt(lhs, ckv_c)
        kn = r[:n_up]
        k2 = jnp.sum((kn * kn).reshape(MLA_HEADS, MLA_NOPE, chunk), axis=1)
        kr_c = kr_buf[slot, :, pl.ds(start, chunk)]
        cos_c = cos_ref[:, pl.ds(start, chunk)]
        sin_c = sin_ref[:, pl.ds(start, chunk)]
        s_rope = _dot(psi1, (kr_c * cos_c).astype(BF16)) + _dot(psi2, (kr_c * sin_c).astype(BF16))
        kr2 = jnp.sum(kr_c * kr_c, axis=0, keepdims=True)
        inv = lax.rsqrt((k2 + kr2) * (1.0 / MLA_QK) + EPS)
        s_scr[:, pl.ds(start, chunk)] = (r[n_up:n_up + MLA_HEADS] + s_rope) * inv
        return 0

    lax.fori_loop(0, t_past // chunk, score_chunk, 0)
    s = s_scr[...]
    m = jnp.maximum(jnp.max(s, axis=-1, keepdims=True), s_new)
    p = jnp.exp(s - m)
    p_new = jnp.exp(s_new - m)
    denom = jnp.sum(p, axis=-1, keepdims=True) + p_new
    s_scr[...] = p

    def pv_chunk(c, acc):
        start = pl.multiple_of(c * chunk, chunk)
        ckv_c = ckv_buf[slot, pl.ds(start, chunk), :].astype(BF16)
        return acc + _dot(s_scr[:, pl.ds(start, chunk)].astype(BF16), ckv_c)

    acc = lax.fori_loop(0, t_past // chunk, pv_chunk, jnp.zeros((MLA_HEADS, MLA_KV_LORA), F32))
    o_ref[0] = (acc + p_new * cnew_ref[0]) / denom


def _mla_decode(page_table, q, qabs16, knew, cnew, gkr, wukt2, cos32, sin32, cache_ckv, cache_kr, chunk):
    nb, n_pages = page_table.shape
    t_past = n_pages * PAGE
    kern = functools.partial(_mla_decode_kernel, n_pages=n_pages, chunk=chunk)
    b3 = lambda b, pt: (b, 0, 0)
    grid_spec = pltpu.PrefetchScalarGridSpec(
        num_scalar_prefetch=1, grid=(nb,),
        in_specs=[
            pl.BlockSpec((1, MLA_HEADS, HEAD_PAD), b3), pl.BlockSpec((1, 16, MLA_KV_LORA), b3),
            pl.BlockSpec((1, MLA_HEADS, HEAD_PAD), b3), pl.BlockSpec((1, 1, MLA_KV_LORA), b3),
            _const_spec((1, HEAD_PAD)), _const_spec((MLA_HEADS * MLA_NOPE, MLA_KV_LORA)),
            _const_spec((MLA_ROPE, t_past)), _const_spec((MLA_ROPE, t_past)),
            pl.BlockSpec(memory_space=pl.ANY), pl.BlockSpec(memory_space=pl.ANY),
        ],
        out_specs=pl.BlockSpec((1, MLA_HEADS, MLA_KV_LORA), b3),
        scratch_shapes=[
            pltpu.VMEM((2, t_past, MLA_KV_LORA), F32), pltpu.VMEM((2, MLA_ROPE, t_past), F32),
            pltpu.VMEM((MLA_HEADS, t_past), F32),
            pltpu.SemaphoreType.DMA((2,)), pltpu.SemaphoreType.DMA((2,)),
        ],
    )
    return pl.pallas_call(
        kern, grid_spec=grid_spec,
        out_shape=jax.ShapeDtypeStruct((nb, MLA_HEADS, MLA_KV_LORA), F32),
        compiler_params=_cparams("arbitrary"), name="mla_decode",
    )(page_table, q, qabs16, knew, cnew, gkr, wukt2, cos32, sin32, cache_ckv, cache_kr)


def _out_ffn_kernel(*refs, n_parts, latent):
    parts = refs[:n_parts]
    x_ref = refs[n_parts]
    idx = n_parts + 1
    if latent:
        wuv_ref = refs[idx]
        idx += 1
    wouts = refs[idx:idx + n_parts]
    gffn_ref, wgu_ref, wd_ref, o_ref, h_scr = refs[idx + n_parts:]
    mix = jnp.zeros(x_ref.shape, F32)
    for p_ref, w_ref in zip(parts, wouts):
        if latent and p_ref is parts[-1]:
            for hd in range(MLA_HEADS):
                lat = p_ref[:, hd * MLA_KV_LORA:(hd + 1) * MLA_KV_LORA].astype(BF16)
                a = _dot(lat, wuv_ref[hd]).astype(BF16)
                mix = mix + _dot(a, w_ref[hd * MLA_V:(hd + 1) * MLA_V, :])
        else:
            mix = mix + _dot(p_ref[...].astype(BF16), w_ref[...])
    x1 = x_ref[...] + mix
    h_scr[...] = _rms(x1, gffn_ref[...]).astype(BF16)
    acc = jnp.zeros_like(x1)
    for c in range(FFN_HIDDEN // FFN_CHUNK):
        lo = c * FFN_CHUNK
        gate = _dot(h_scr[...], wgu_ref[:, lo:lo + FFN_CHUNK])
        up = _dot(h_scr[...], wgu_ref[:, FFN_HIDDEN + lo:FFN_HIDDEN + lo + FFN_CHUNK])
        act = (gate / (1.0 + jnp.exp(-gate)) * up).astype(BF16)
        acc = acc + _dot(act, wd_ref[lo:lo + FFN_CHUNK, :])
    o_ref[...] = x1 + acc


def _out_ffn(parts, x, wouts, gffn, wgu, wd, tm, wuv=None):
    m = x.shape[0]
    row = lambda i: (i, 0)
    in_specs = [pl.BlockSpec((tm, p.shape[1]), row) for p in parts] + [pl.BlockSpec((tm, D_MODEL), row)]
    args = list(parts) + [x]
    if wuv is not None:
        in_specs.append(_const_spec(wuv.shape))
        args.append(wuv)
    in_specs += [_const_spec(w.shape) for w in wouts]
    in_specs += [_const_spec((1, D_MODEL)), _const_spec(wgu.shape), _const_spec(wd.shape)]
    args += list(wouts) + [gffn, wgu, wd]
    return pl.pallas_call(
        functools.partial(_out_ffn_kernel, n_parts=len(parts), latent=wuv is not None),
        grid=(m // tm,), in_specs=in_specs,
        out_specs=pl.BlockSpec((tm, D_MODEL), row),
        out_shape=jax.ShapeDtypeStruct((m, D_MODEL), F32),
        scratch_shapes=[pltpu.VMEM((tm, D_MODEL), BF16)],
        compiler_params=_cparams("parallel"), name="out_ffn",
    )(*args)


def _prep_c_kernel(x_ref, gmix_ref, wqkv_ref, gq_ref, gk_ref, seg_ref, ca_ref, cb1_ref, cb2_ref,
                   kf_ref, vf_ref, q_ref, k_ref, v_ref):
    h = _rms(x_ref[...], gmix_ref[...]).astype(BF16)
    qkv = _dot(h, wqkv_ref[...])
    ca = ca_ref[...]
    cb1 = cb1_ref[...]
    cb2 = cb2_ref[...]
    seg = seg_ref[...]

    def norm_rope(t, g):
        ms = _dot((t * t).astype(BF16), seg) * (1.0 / C_HEAD_DIM)
        t = t * lax.rsqrt(ms + EPS) * g
        return t * ca + pltpu.roll(t, LANES - C_ROT // 2, 1) * cb1 + pltpu.roll(t, C_ROT // 2, 1) * cb2

    lane = lax.broadcasted_iota(jnp.int32, (1, LANES), 1)
    low = lane < C_HEAD_DIM
    for j in range(C_Q_WIDTH // LANES):
        qt = norm_rope(qkv[:, j * LANES:(j + 1) * LANES], gq_ref[...])
        q_ref[j] = (qt * C_SCALE).astype(BF16)
    for j in range(C_KV_WIDTH // LANES):
        kt = norm_rope(qkv[:, C_Q_WIDTH + j * LANES:C_Q_WIDTH + (j + 1) * LANES], gk_ref[...])
        vt = qkv[:, C_Q_WIDTH + C_KV_WIDTH + j * LANES:C_Q_WIDTH + C_KV_WIDTH + (j + 1) * LANES]
        kf_ref[:, j * LANES:(j + 1) * LANES] = kt
        vf_ref[:, j * LANES:(j + 1) * LANES] = vt
        k_ref[2 * j] = jnp.where(low, kt, 0.0).astype(BF16)
        k_ref[2 * j + 1] = jnp.where(low, 0.0, kt).astype(BF16)
        v_ref[2 * j] = jnp.where(low, vt, 1.0).astype(BF16)
        v_ref[2 * j + 1] = jnp.where(low, 1.0, vt).astype(BF16)


def _prep_c(x, gmix, wc, tabs, tm):
    m = x.shape[0]
    n_tab = tabs[0].shape[0] // tm
    row = lambda i: (i, 0)
    tab = lambda i: (i % n_tab, 0)
    nq = C_Q_WIDTH // LANES
    return pl.pallas_call(
        _prep_c_kernel,
        grid=(m // tm,),
        in_specs=[
            pl.BlockSpec((tm, D_MODEL), row), _const_spec((1, D_MODEL)),
            _const_spec((D_MODEL, C_Q_WIDTH + 2 * C_KV_WIDTH)),
            _const_spec((1, LANES)), _const_spec((1, LANES)), _const_spec((LANES, LANES)),
            pl.BlockSpec((tm, LANES), tab), pl.BlockSpec((tm, LANES), tab), pl.BlockSpec((tm, LANES), tab),
        ],
        out_specs=[
            pl.BlockSpec((tm, C_KV_WIDTH), row), pl.BlockSpec((tm, C_KV_WIDTH), row),
            pl.BlockSpec((nq, tm, LANES), lambda i: (0, i, 0)),
            pl.BlockSpec((C_KV_HEADS, tm, LANES), lambda i: (0, i, 0)),
            pl.BlockSpec((C_KV_HEADS, tm, LANES), lambda i: (0, i, 0)),
        ],
        out_shape=[
            jax.ShapeDtypeStruct((m, C_KV_WIDTH), F32), jax.ShapeDtypeStruct((m, C_KV_WIDTH), F32),
            jax.ShapeDtypeStruct((nq, m, LANES), BF16),
            jax.ShapeDtypeStruct((C_KV_HEADS, m, LANES), BF16),
            jax.ShapeDtypeStruct((C_KV_HEADS, m, LANES), BF16),
        ],
        compiler_params=_cparams("parallel"), name="prep_c",
    )(x, gmix, wc["w_qkv"], wc["g_q"], wc["g_k"], wc["seg"], *tabs)


def _topk_rank_mask(g, valid, n_rows):
    row = lax.broadcasted_iota(jnp.int32, g.shape, 0)
    g = jnp.where(valid, g, NEG)
    cnt = jnp.zeros(g.shape, F32)
    for mrow in range(n_rows):
        gm = g[mrow:mrow + 1, :]
        cnt = cnt + jnp.where(row > mrow, jnp.where(gm >= g, 1.0, 0.0), jnp.where(gm > g, 1.0, 0.0))
    return jnp.where(valid & (cnt < MOBA_TOPK), 1.0, 0.0)


def _moba_prompt_kernel(q_ref, k_ref, v_ref, o_ref, m_scr, acc_scr, *, n_blk):
    qi = pl.program_id(1)
    blk = MOBA_BLOCK
    rows = C_GROUP * blk
    r_i = lax.broadcasted_iota(jnp.int32, (blk, blk), 0)
    c_i = lax.broadcasted_iota(jnp.int32, (blk, blk), 1)
    eye = jnp.where(r_i == c_i, 1.0, 0.0).astype(BF16)
    q_in_blk = lax.broadcasted_iota(jnp.int32, (rows, blk), 0) & (blk - 1)
    causal = lax.broadcasted_iota(jnp.int32, (rows, blk), 1) <= q_in_blk
    lane = lax.broadcasted_iota(jnp.int32, (1, LANES), 1)
    low = lane < C_HEAD_DIM
    cand = lax.broadcasted_iota(jnp.int32, (8, rows), 0)
    cand_k = lax.broadcasted_iota(jnp.int32, (8, LANES), 0)
    outs = []
    for kv in range(C_KV_HEADS):
        pair = kv // 2
        q = q_ref[pair * C_GROUP:(pair + 1) * C_GROUP].reshape(rows, LANES)
        kmean = jnp.zeros((8, LANES), F32)
        for n in range(n_blk):
            ksum = jnp.sum(k_ref[kv, n * blk:(n + 1) * blk, :].astype(F32), axis=0, keepdims=True)
            kmean = jnp.where(cand_k == n, ksum * (1.0 / blk), kmean)
        k_hi = kmean.astype(BF16)
        k_lo = (kmean - k_hi.astype(F32)).astype(BF16)
        gate_t = _dot_nt(k_hi, q) + _dot_nt(k_lo, q)
        sel_t = _topk_rank_mask(gate_t, cand < qi, 8).astype(BF16)
        sel = jnp.concatenate(
            [_dot_nt(eye, sel_t[:, g * blk:(g + 1) * blk]) for g in range(C_GROUP)], axis=0)

        m_scr[...] = jnp.full((rows, 1), NEG, F32)
        acc_scr[...] = jnp.zeros((rows, LANES), F32)

        def attend(s, v_blk):
            m_prev = m_scr[...]
            m_new = jnp.maximum(m_prev, jnp.max(s, axis=-1, keepdims=True))
            p = jnp.exp(s - m_new)
            acc_scr[...] = jnp.exp(m_prev - m_new) * acc_scr[...] + _dot(p.astype(BF16), v_blk)
            m_scr[...] = m_new

        for n in range(n_blk - 1):
            @pl.when(n < qi)
            def _():
                s = _dot_nt(q, k_ref[kv, n * blk:(n + 1) * blk, :])
                attend(s + (sel[:, n:n + 1] - 1.0) * (-NEG), v_ref[kv, n * blk:(n + 1) * blk, :])

        start = pl.multiple_of(qi * blk, blk)
        s = _dot_nt(q, k_ref[kv, pl.ds(start, blk), :])
        attend(jnp.where(causal, s, NEG), v_ref[kv, pl.ds(start, blk), :])
        acc = acc_scr[...]
        outs.append(acc / pltpu.roll(acc, C_HEAD_DIM, 1))
    for pair in range(C_KV_HEADS // 2):
        o = jnp.where(low, outs[2 * pair], outs[2 * pair + 1]).astype(BF16)
        for g in range(C_GROUP):
            tile = pair * C_GROUP + g
            o_ref[:, tile * LANES:(tile + 1) * LANES] = o[g * blk:(g + 1) * blk]


def _moba_prompt(q, k, v, seq):
    nq_t, m, _ = q.shape
    n_blk = seq // MOBA_BLOCK
    assert n_blk <= 8
    rows = C_GROUP * MOBA_BLOCK
    return pl.pallas_call(
        functools.partial(_moba_prompt_kernel, n_blk=n_blk),
        grid=(m // seq, n_blk),
        in_specs=[
            pl.BlockSpec((nq_t, MOBA_BLOCK, LANES), lambda b, i: (0, b * n_blk + i, 0)),
            pl.BlockSpec((C_KV_HEADS, seq, LANES), lambda b, i: (0, b, 0)),
            pl.BlockSpec((C_KV_HEADS, seq, LANES), lambda b, i: (0, b, 0)),
        ],
        out_specs=pl.BlockSpec((MOBA_BLOCK, nq_t * LANES), lambda b, i: (b * n_blk + i, 0)),
        out_shape=jax.ShapeDtypeStruct((m, nq_t * LANES), BF16),
        scratch_shapes=[pltpu.VMEM((rows, 1), F32), pltpu.VMEM((rows, LANES), F32)],
        compiler_params=_cparams("parallel", "arbitrary"), name="moba_prompt",
    )(q, k, v)


def _moba_decode_kernel(pt_ref, qbd_ref, knew_ref, vnew_ref, k_hbm, v_hbm, o_ref,
                        k_buf, v_buf, s_scr, sem_k, sem_v, *, n_pages, chunk):
    t_past = n_pages * PAGE
    n_blk = t_past // MOBA_BLOCK
    slot = _paged_prefetch(pt_ref, n_pages, (k_hbm, v_hbm), (k_buf, v_buf), (sem_k, sem_v), (1, 1))
    qbd = qbd_ref[0]
    s_new = jnp.sum(qbd.astype(F32) * knew_ref[0], axis=-1, keepdims=True)

    def score_chunk(c, _):
        start = pl.multiple_of(c * chunk, chunk)
        s_scr[:, pl.ds(start, chunk)] = _dot(qbd, k_buf[slot, :, pl.ds(start, chunk)].astype(BF16))
        return 0

    lax.fori_loop(0, t_past // chunk, score_chunk, 0)

    gates, maxes = [], []
    for n in range(n_blk):
        s_n = s_scr[:, n * MOBA_BLOCK:(n + 1) * MOBA_BLOCK]
        gates.append(jnp.sum(s_n, axis=-1, keepdims=True))
        maxes.append(jnp.max(s_n, axis=-1, keepdims=True))
    sels = []
    for n in range(n_blk):
        cnt = jnp.zeros_like(gates[n])
        for mb in range(n_blk):
            if mb != n:
                beats = (gates[mb] >= gates[n]) if mb < n else (gates[mb] > gates[n])
                cnt = cnt + jnp.where(beats, 1.0, 0.0)
        sels.append(cnt < MOBA_TOPK)
    m = s_new
    for n in range(n_blk):
        m = jnp.maximum(m, jnp.where(sels[n], maxes[n], NEG))
    p_new = jnp.exp(s_new - m)
    psum = jnp.zeros((C_HEADS, MOBA_BLOCK), F32)
    for n in range(n_blk):
        sl = slice(n * MOBA_BLOCK, (n + 1) * MOBA_BLOCK)
        p = jnp.where(sels[n], jnp.exp(s_scr[:, sl] - m), 0.0)
        s_scr[:, sl] = p
        psum = psum + p
    denom = jnp.sum(psum, axis=-1, keepdims=True) + p_new

    def pv_chunk(c, acc):
        start = pl.multiple_of(c * chunk, chunk)
        return acc + _dot_nt(s_scr[:, pl.ds(start, chunk)].astype(BF16),
                             v_buf[slot, :, pl.ds(start, chunk)].astype(BF16))

    acc = lax.fori_loop(0, t_past // chunk, pv_chunk, jnp.zeros((C_HEADS, C_KV_WIDTH), F32))
    o_ref[0] = (acc + p_new * vnew_ref[0]) / denom


def _moba_decode(page_table, qbd, knew, vnew, cache_k, cache_v, chunk):
    nb, n_pages = page_table.shape
    t_past = n_pages * PAGE
    b3 = lambda b, pt: (b, 0, 0)
    grid_spec = pltpu.PrefetchScalarGridSpec(
        num_scalar_prefetch=1, grid=(nb,),
        in_specs=[
            pl.BlockSpec((1, C_HEADS, C_KV_WIDTH), b3), pl.BlockSpec((1, 1, C_KV_WIDTH), b3),
            pl.BlockSpec((1, 1, C_KV_WIDTH), b3),
            pl.BlockSpec(memory_space=pl.ANY), pl.BlockSpec(memory_space=pl.ANY),
        ],
        out_specs=pl.BlockSpec((1, C_HEADS, C_KV_WIDTH), b3),
        scratch_shapes=[
            pltpu.VMEM((2, C_KV_WIDTH, t_past), F32), pltpu.VMEM((2, C_KV_WIDTH, t_past), F32),
            pltpu.VMEM((C_HEADS, t_past), F32),
            pltpu.SemaphoreType.DMA((2,)), pltpu.SemaphoreType.DMA((2,)),
        ],
    )
    return pl.pallas_call(
        functools.partial(_moba_decode_kernel, n_pages=n_pages, chunk=chunk), grid_spec=grid_spec,
        out_shape=jax.ShapeDtypeStruct((nb, C_HEADS, C_KV_WIDTH), F32),
        compiler_params=_cparams("arbitrary"), name="moba_decode",
    )(page_table, qbd, knew, vnew, cache_k, cache_v)


def _rope_angles(pos, dim, theta):
    inv_freq = jnp.exp(jnp.arange(0, dim, 2, dtype=F32) * (-math.log(theta) / dim))
    ang = pos.astype(F32)[:, None] * inv_freq[None, :]
    return jnp.cos(ang), jnp.sin(ang)


def _mla_tables(pos):
    cos, sin = _rope_angles(pos, MLA_ROPE, MLA_THETA)
    z = lambda n: jnp.zeros((pos.shape[0], n), F32)
    half = MLA_ROPE // 2
    tail = HEAD_PAD - MLA_QK
    cos_t = jnp.concatenate([z(MLA_NOPE), cos, cos, z(tail)], axis=1)
    sina = jnp.concatenate([z(MLA_NOPE), -sin, z(half), z(tail)], axis=1)
    sinb = jnp.concatenate([z(MLA_NOPE), z(half), sin, z(tail)], axis=1)
    return cos_t, sina, sinb


def _moba_tables(pos):
    cos, sin = _rope_angles(pos, C_ROT, C_THETA)
    n = pos.shape[0]
    half = C_ROT // 2
    rest = C_HEAD_DIM - C_ROT
    one = jnp.ones((n, rest), F32)
    z = lambda k: jnp.zeros((n, k), F32)
    ca = jnp.concatenate([cos, cos, one] * 2, axis=1)
    cb1 = jnp.concatenate([-sin, z(half), z(rest)] * 2, axis=1)
    cb2 = jnp.concatenate([z(half), sin, z(rest)] * 2, axis=1)
    return ca, cb1, cb2


def _pad_lanes(a, start, total):
    pad = [(0, 0)] * (a.ndim - 1) + [(start, total - start - a.shape[-1])]
    return jnp.pad(a, pad)


def _layer_a_weights(w_in, g_q_lora, g_kv_lora, w_uq, w_uk, w_uv, g_q, g_k, w_pool, s_pool, w_out):
    o_kr = POOL_WIDTH + MLA_Q_LORA + MLA_KV_LORA
    w_in_p = jnp.concatenate([w_in[:, :o_kr], _pad_lanes(w_in[:, o_kr:], MLA_NOPE, HEAD_PAD)], axis=1)
    hp = MLA_HEADS * HEAD_PAD
    return {
        "w_in": w_in_p.astype(BF16),
        "g_q_lora": g_q_lora[None], "g_kv_lora": g_kv_lora[None],
        "w_uq": _pad_lanes(w_uq, 0, HEAD_PAD).reshape(MLA_Q_LORA, hp).astype(BF16),
        "w_uk": _pad_lanes(w_uk, 0, HEAD_PAD).reshape(MLA_KV_LORA, hp).astype(BF16),
        "w_uv": _pad_lanes(w_uv, 0, HEAD_PAD).reshape(MLA_KV_LORA, hp).astype(BF16),
        "w_uk_t": _pad_lanes(jnp.transpose(w_uk, (1, 0, 2)), 0, HEAD_PAD).transpose(0, 2, 1).astype(BF16),
        "w_uk_t2": jnp.transpose(w_uk, (1, 2, 0)).reshape(MLA_HEADS * MLA_NOPE, MLA_KV_LORA).astype(BF16),
        "w_uv_h": jnp.transpose(w_uv, (1, 0, 2)).astype(BF16),
        "g_q": _pad_lanes(g_q, 0, HEAD_PAD)[None],
        "g_kn": _pad_lanes(g_k[:MLA_NOPE], 0, HEAD_PAD)[None],
        "g_kr": _pad_lanes(g_k[MLA_NOPE:], MLA_NOPE, HEAD_PAD)[None],
        "w_pool": w_pool.astype(BF16), "s_pool": s_pool[None],
        "w_out_pool": w_out[:POOL_WIDTH].astype(BF16), "w_out_mla": w_out[POOL_WIDTH:].astype(BF16),
    }


def _moba_tile_heads():
    order = []
    for j in range(C_Q_WIDTH // LANES):
        p, g = divmod(j, C_GROUP)
        order += [(2 * p) * C_GROUP + g, (2 * p + 1) * C_GROUP + g]
    return order


def _layer_c_weights(w_qkv, g_q, g_k, w_o):
    order = jnp.array(_moba_tile_heads())
    wq = w_qkv[:, :C_Q_WIDTH].reshape(D_MODEL, C_HEADS, C_HEAD_DIM)[:, order].reshape(D_MODEL, C_Q_WIDTH)
    w_o_p = w_o.reshape(C_HEADS, C_HEAD_DIM, D_MODEL)[order].reshape(C_Q_WIDTH, D_MODEL)
    seg = jnp.kron(jnp.eye(LANES // C_HEAD_DIM, dtype=F32), jnp.ones((C_HEAD_DIM, C_HEAD_DIM), F32))
    return {
        "w_qkv": jnp.concatenate([wq, w_qkv[:, C_Q_WIDTH:]], axis=1).astype(BF16),
        "g_q": jnp.tile(g_q, 2)[None], "g_k": jnp.tile(g_k, 2)[None], "seg": seg.astype(BF16),
        "w_o": w_o_p.astype(BF16),
    }


def _pages_feature_major(cache):
    n = cache.shape[0]
    return jnp.transpose(cache, (0, 2, 3, 1)).reshape(n, C_KV_WIDTH, PAGE)


def _tiles_to_rows(t):
    return jnp.transpose(t, (1, 0, 2)).reshape(t.shape[1], t.shape[0] * LANES)


def _pick_tile(m, pref):
    return pref if m % pref == 0 else m


def kernel(x_prompt, x_sample, cache_mla_ckv, cache_mla_krope, state_pool, cache_moba_k, cache_moba_v,
           page_table, g_mix, g_ffn, w_in_a, g_q_lora, g_kv_lora, w_uq, w_uk, w_uv, g_mla_q, g_mla_k,
           w_pool, s_pool, w_out_a, w_qkv_c, g_moba_q, g_moba_k, w_o_c, w_gate_up, w_down):
    bp, seq, _ = x_prompt.shape
    bs = x_sample.shape[0]
    n_pages = page_table.shape[1]
    past = n_pages * PAGE
    depth = g_mix.shape[0]
    mp = bp * seq
    xp = x_prompt.reshape(mp, D_MODEL)
    xs = x_sample.reshape(bs, D_MODEL)
    tm_p = _pick_tile(mp, 256)
    tm_f = _pick_tile(mp, 512)
    tq = 256
    chunk = min(1024, past)

    pos_p = jnp.arange(seq, dtype=jnp.int32)
    pos_s = jnp.full((bs,), past, dtype=jnp.int32)
    wgu = w_gate_up.astype(BF16)
    wdn = w_down.astype(BF16)

    outs_p = {k: [] for k in ("ckv", "kr", "pool", "k", "v")}
    outs_s = {k: [] for k in ("ckv", "kr", "pool", "k", "v")}
    for layer in range(depth):
        i = layer // 2
        gmix = g_mix[layer][None]
        gffn = g_ffn[layer][None]
        if layer % 2 == 0:
            wa = _layer_a_weights(w_in_a[i], g_q_lora[i], g_kv_lora[i], w_uq[i], w_uk[i], w_uv[i],
                                  g_mla_q[i], g_mla_k[i], w_pool[i], s_pool[i], w_out_a[i])
            u, ckv, kr, q, k, v = _prep_a(xp, gmix, wa, _mla_tables(pos_p), tm_p, decode=False)
            pool_out = _pool_prompt(u, wa["w_pool"], wa["s_pool"], seq, tm_p)
            mla_out = _mla_prompt(q, k, v, seq, tq)
            xp = _out_ffn([pool_out, mla_out], xp, [wa["w_out_pool"], wa["w_out_mla"]], gffn,
                          wgu[layer], wdn[layer], tm_f)
            outs_p["ckv"].append(ckv.reshape(bp, seq, MLA_KV_LORA))
            outs_p["kr"].append(kr.reshape(bp, seq, MLA_ROPE))
            outs_p["pool"].append(u.reshape(bp, seq, POOL_WIDTH)[:, seq - POOL_STATE:])
            u, ckv, kr, q, k, v, qabs = _prep_a(xs, gmix, wa, _mla_tables(pos_s), bs, decode=True)
            state = state_pool[i]
            pool_out = _pool_sample(jnp.transpose(state, (1, 0, 2)), u, wa["w_pool"], wa["s_pool"])
            cos_k, sin_k = _rope_angles(jnp.arange(past, dtype=jnp.int32), MLA_ROPE, MLA_THETA)
            qabs16 = jnp.pad(qabs.reshape(bs, MLA_HEADS, MLA_KV_LORA), ((0, 0), (0, 16 - MLA_HEADS), (0, 0)))
            o_lat = _mla_decode(
                page_table, q.reshape(bs, MLA_HEADS, HEAD_PAD), qabs16, k.reshape(bs, MLA_HEADS, HEAD_PAD),
                ckv.reshape(bs, 1, MLA_KV_LORA), wa["g_kr"], wa["w_uk_t2"],
                jnp.concatenate([cos_k, cos_k], axis=1).T, jnp.concatenate([sin_k, sin_k], axis=1).T,
                cache_mla_ckv[i], jnp.transpose(cache_mla_krope[i], (0, 2, 1)), chunk)
            xs = _out_ffn([pool_out, o_lat.reshape(bs, MLA_HEADS * MLA_KV_LORA)], xs,
                          [wa["w_out_pool"], wa["w_out_mla"]], gffn, wgu[layer], wdn[layer], bs,
                          wuv=wa["w_uv_h"])
            outs_s["ckv"].append(ckv.reshape(bs, 1, MLA_KV_LORA))
            outs_s["kr"].append(kr.reshape(bs, 1, MLA_ROPE))
            outs_s["pool"].append(jnp.concatenate([state[:, 1:], u[:, None]], axis=1))
        else:
            wc = _layer_c_weights(w_qkv_c[i], g_moba_q[i], g_moba_k[i], w_o_c[i])
            kf, vf, q, k, v = _prep_c(xp, gmix, wc, _moba_tables(pos_p), tm_p)
            o = _moba_prompt(q, k, v, seq)
            xp = _out_ffn([o], xp, [wc["w_o"]], gffn, wgu[layer], wdn[layer], tm_f)
            outs_p["k"].append(kf.reshape(bp, seq, C_KV_HEADS, C_HEAD_DIM))
            outs_p["v"].append(vf.reshape(bp, seq, C_KV_HEADS, C_HEAD_DIM))
            kf, vf, q, k, v = _prep_c(xs, gmix, wc, _moba_tables(pos_s), bs)
            qh = _tiles_to_rows(q).reshape(bs, C_Q_WIDTH // LANES, 2, C_HEAD_DIM)
            kv_of = jnp.array(_moba_tile_heads()).reshape(-1, 2) // C_GROUP
            onehot = (kv_of[:, :, None] == jnp.arange(C_KV_HEADS)[None, None, :]).astype(BF16)
            qbd = (qh[:, :, :, None, :] * onehot[None, :, :, :, None]).reshape(bs, C_HEADS, C_KV_WIDTH)
            o = _moba_decode(page_table, qbd, kf.reshape(bs, 1, C_KV_WIDTH), vf.reshape(bs, 1, C_KV_WIDTH),
                             _pages_feature_major(cache_moba_k[i]), _pages_feature_major(cache_moba_v[i]), chunk)
            o = (o.reshape(bs, C_Q_WIDTH // LANES, 2, C_KV_HEADS, C_HEAD_DIM)
                 * onehot[None, :, :, :, None].astype(F32)).sum(axis=3).reshape(bs, C_Q_WIDTH)
            xs = _out_ffn([o], xs, [wc["w_o"]], gffn, wgu[layer], wdn[layer], bs)
            outs_s["k"].append(kf.reshape(bs, 1, C_KV_HEADS, C_HEAD_DIM))
            outs_s["v"].append(vf.reshape(bs, 1, C_KV_HEADS, C_HEAD_DIM))

    st = lambda xs_: jnp.stack(xs_)
    return (xp.reshape(bp, seq, D_MODEL), xs.reshape(bs, 1, D_MODEL),
            st(outs_p["ckv"]), st(outs_p["kr"]), st(outs_p["pool"]), st(outs_p["k"]), st(outs_p["v"]),
            st(outs_s["ckv"]), st(outs_s["kr"]), st(outs_s["pool"]), st(outs_s["k"]), st(outs_s["v"]))
```

```python
import functools
import math

import jax
import jax.numpy as jnp
from jax import lax
from jax.experimental import pallas as pl
from jax.experimental.pallas import tpu as pltpu

F32 = jnp.float32
BF16 = jnp.bfloat16
LOG2E = math.log2(math.e)

D_MODEL = 1024
EPS = 1e-6
NEG = -1e30

POOL_WIDTH = 512
POOL_GROUPS = 4
POOL_GROUP_DIM = 128
POOL_WINDOWS = (2, 4, 8, 16)
POOL_STATE = 15
POOL_HALO = 16

MLA_HEADS = 8
MLA_NOPE = 64
MLA_ROPE = 32
MLA_QK = 96
MLA_V = 64
MLA_Q_LORA = 384
MLA_KV_LORA = 256
MLA_THETA = 10000.0
MLA_SCALE = MLA_QK ** -0.5
HEAD_PAD = 128
MLA_TILES = (1024, 512, 256)
A_IN_PAD = POOL_WIDTH + MLA_Q_LORA + MLA_KV_LORA + HEAD_PAD

C_HEADS = 16
C_KV_HEADS = 4
C_GROUP = 4
C_HEAD_DIM = 64
C_ROT = 16
C_THETA = 500000.0
C_SCALE = C_HEAD_DIM ** -0.5
MOBA_BLOCK = 256
MOBA_TOPK = 3
C_Q_WIDTH = C_HEADS * C_HEAD_DIM
C_KV_WIDTH = C_KV_HEADS * C_HEAD_DIM

FFN_HIDDEN = 2816
FFN_CHUNK = 256

PAGE = 128
LANES = 128
VMEM_LIMIT = 56 * 1024 * 1024


def _cparams(*sem):
    return pltpu.CompilerParams(dimension_semantics=sem, vmem_limit_bytes=VMEM_LIMIT)


def _const_spec(shape):
    nd = len(shape)
    return pl.BlockSpec(shape, lambda *_: (0,) * nd, pipeline_mode=pl.Buffered(1))


def _rms(x, g):
    ms = jnp.mean(x * x, axis=-1, keepdims=True)
    return x * lax.rsqrt(ms + EPS) * g


def _dot(a, b):
    return jnp.dot(a, b, preferred_element_type=F32)


def _dot_nt(a, b):
    return lax.dot_general(a, b, (((1,), (1,)), ((), ())), preferred_element_type=F32)


def _prep_a_kernel(x_ref, gmix_ref, win_ref, gql_ref, gkvl_ref, wuq_ref, gq_ref, gkn_ref, gkr_ref,
                   wuk_ref, wuv_ref, cos_ref, sina_ref, sinb_ref, *rest, decode):
    if decode:
        wukt_ref, u_ref, ckv_ref, kr_ref, q_ref, k_ref, vt_ref, qabs_ref = rest
    else:
        u_ref, ckv_ref, kr_ref, q_ref, k_ref, vt_ref = rest
    h = _rms(x_ref[...], gmix_ref[...]).astype(BF16)
    hw = _dot(h, win_ref[...])
    u_ref[...] = hw[:, :POOL_WIDTH]
    o_q = POOL_WIDTH
    o_kv = o_q + MLA_Q_LORA
    o_kr = o_kv + MLA_KV_LORA
    cq = _rms(hw[:, o_q:o_kv], gql_ref[...]).astype(BF16)
    ckv = _rms(hw[:, o_kv:o_kr], gkvl_ref[...])
    ckv_ref[...] = ckv
    kr = hw[:, o_kr:o_kr + HEAD_PAD]
    kr_ref[...] = kr[:, MLA_NOPE:MLA_QK]
    cos = cos_ref[...]
    sina = sina_ref[...]
    sinb = sinb_ref[...]

    def rope(t, base):
        return t * base + pltpu.roll(t, HEAD_PAD - 16, 1) * sina + pltpu.roll(t, 16, 1) * sinb

    q = _dot(cq, wuq_ref[...])
    gq = gq_ref[...]
    qbase = gkn_ref[...] + cos
    for hd in range(MLA_HEADS):
        sl = slice(hd * HEAD_PAD, (hd + 1) * HEAD_PAD)
        blk = q[:, sl]
        ms = jnp.sum(blk * blk, axis=-1, keepdims=True) * (1.0 / MLA_QK)
        qt = (rope(blk * lax.rsqrt(ms + EPS) * gq, qbase) * (MLA_SCALE * LOG2E)).astype(BF16)
        q_ref[:, sl] = qt
        if decode:
            qabs_ref[:, hd * MLA_KV_LORA:(hd + 1) * MLA_KV_LORA] = _dot(qt, wukt_ref[hd]).astype(BF16)

    ckv_b = ckv.astype(BF16)
    kn = _dot(ckv_b, wuk_ref[...])
    krsq = jnp.sum(kr * kr, axis=-1, keepdims=True)
    krot = rope(kr * gkr_ref[...], cos)
    for hd in range(MLA_HEADS):
        sl = slice(hd * HEAD_PAD, (hd + 1) * HEAD_PAD)
        blk = kn[:, sl]
        ms = (jnp.sum(blk * blk, axis=-1, keepdims=True) + krsq) * (1.0 / MLA_QK)
        k_ref[:, sl] = ((blk + krot) * lax.rsqrt(ms + EPS)).astype(BF16)
    lane = lax.broadcasted_iota(jnp.int32, (1, HEAD_PAD), 1)
    vv = _dot(ckv_b, wuv_ref[...])
    for hd in range(MLA_HEADS):
        vt_ref[hd] = jnp.where(lane < MLA_V, vv[:, hd * HEAD_PAD:(hd + 1) * HEAD_PAD], 1.0).T.astype(BF16)


def _prep_a(x, gmix, wa, tabs, tm, decode):
    m = x.shape[0]
    n_tab = tabs[0].shape[0] // tm
    row = lambda i: (i, 0)
    tab = lambda i: (i % n_tab, 0)
    hp = MLA_HEADS * HEAD_PAD
    in_specs = [
        pl.BlockSpec((tm, D_MODEL), row), _const_spec((1, D_MODEL)), _const_spec((D_MODEL, A_IN_PAD)),
        _const_spec((1, MLA_Q_LORA)), _const_spec((1, MLA_KV_LORA)), _const_spec((MLA_Q_LORA, hp)),
        _const_spec((1, HEAD_PAD)), _const_spec((1, HEAD_PAD)), _const_spec((1, HEAD_PAD)),
        _const_spec((MLA_KV_LORA, hp)), _const_spec((MLA_KV_LORA, hp)),
        pl.BlockSpec((tm, HEAD_PAD), tab), pl.BlockSpec((tm, HEAD_PAD), tab), pl.BlockSpec((tm, HEAD_PAD), tab),
    ]
    args = [x, gmix, wa["w_in"], wa["g_q_lora"], wa["g_kv_lora"], wa["w_uq"], wa["g_q"], wa["g_kn"], wa["g_kr"],
            wa["w_uk"], wa["w_uv"], *tabs]
    out_shape = [
        jax.ShapeDtypeStruct((m, POOL_WIDTH), F32), jax.ShapeDtypeStruct((m, MLA_KV_LORA), F32),
        jax.ShapeDtypeStruct((m, MLA_ROPE), F32), jax.ShapeDtypeStruct((m, hp), BF16),
        jax.ShapeDtypeStruct((m, hp), BF16), jax.ShapeDtypeStruct((MLA_HEADS, HEAD_PAD, m), BF16),
    ]
    out_specs = [
        pl.BlockSpec((tm, POOL_WIDTH), row), pl.BlockSpec((tm, MLA_KV_LORA), row),
        pl.BlockSpec((tm, MLA_ROPE), row), pl.BlockSpec((tm, hp), row),
        pl.BlockSpec((tm, hp), row), pl.BlockSpec((MLA_HEADS, HEAD_PAD, tm), lambda i: (0, 0, i)),
    ]
    if decode:
        in_specs.append(_const_spec((MLA_HEADS, HEAD_PAD, MLA_KV_LORA)))
        args.append(wa["w_uk_t"])
        out_shape.append(jax.ShapeDtypeStruct((m, MLA_HEADS * MLA_KV_LORA), BF16))
        out_specs.append(pl.BlockSpec((tm, MLA_HEADS * MLA_KV_LORA), row))
    return pl.pallas_call(
        functools.partial(_prep_a_kernel, decode=decode),
        grid=(m // tm,), in_specs=in_specs, out_specs=out_specs, out_shape=out_shape,
        compiler_params=_cparams("parallel"), name="prep_a_dec" if decode else "prep_a",
    )(*args)


def _pool_mix(diffs, wp_ref, sp_ref, o_ref):
    for g in range(POOL_GROUPS):
        sl = slice(g * POOL_GROUP_DIM, (g + 1) * POOL_GROUP_DIM)
        o_ref[:, sl] = _dot(diffs[g].astype(BF16), wp_ref[g]) * sp_ref[:, sl]


def _pool_prompt_kernel(u_ref, halo_ref, wp_ref, sp_ref, o_ref, ext_ref):
    j = pl.program_id(1)
    tm = u_ref.shape[0]
    ext_ref[:POOL_HALO, :] = jnp.where(j > 0, halo_ref[...], 0.0)
    ext_ref[POOL_HALO:, :] = u_ref[...]
    pos = j * tm + lax.broadcasted_iota(jnp.int32, (tm, POOL_GROUP_DIM), 0)
    diffs = []
    for g, w in enumerate(POOL_WINDOWS):
        sl = slice(g * POOL_GROUP_DIM, (g + 1) * POOL_GROUP_DIM)
        cur = ext_ref[POOL_HALO:, sl]
        acc = cur
        for k in range(1, w):
            acc = acc + ext_ref[POOL_HALO - k:POOL_HALO - k + tm, sl]
        cnt = jnp.minimum(pos + 1, w).astype(F32)
        diffs.append(acc / cnt - cur)
    _pool_mix(diffs, wp_ref, sp_ref, o_ref)


def _pool_prompt(u, wp, sp, seq, tm):
    m = u.shape[0]
    nj = seq // tm
    hb = tm // POOL_HALO
    return pl.pallas_call(
        _pool_prompt_kernel,
        grid=(m // seq, nj),
        in_specs=[
            pl.BlockSpec((tm, POOL_WIDTH), lambda b, j: (b * nj + j, 0)),
            pl.BlockSpec((POOL_HALO, POOL_WIDTH), lambda b, j: (jnp.maximum((b * nj + j) * hb - 1, 0), 0)),
            _const_spec((POOL_GROUPS, POOL_GROUP_DIM, POOL_GROUP_DIM)), _const_spec((1, POOL_WIDTH)),
        ],
        out_specs=pl.BlockSpec((tm, POOL_WIDTH), lambda b, j: (b * nj + j, 0)),
        out_shape=jax.ShapeDtypeStruct((m, POOL_WIDTH), F32),
        scratch_shapes=[pltpu.VMEM((tm + POOL_HALO, POOL_WIDTH), F32)],
        compiler_params=_cparams("parallel", "parallel"), name="pool_prompt",
    )(u, u, wp, sp)


def _pool_sample_kernel(st_ref, u_ref, wp_ref, sp_ref, o_ref):
    diffs = []
    for g, w in enumerate(POOL_WINDOWS):
        sl = slice(g * POOL_GROUP_DIM, (g + 1) * POOL_GROUP_DIM)
        cur = u_ref[:, sl]
        acc = cur
        for k in range(1, w):
            acc = acc + st_ref[POOL_STATE - k, :, sl]
        diffs.append(acc / float(w) - cur)
    _pool_mix(diffs, wp_ref, sp_ref, o_ref)


def _pool_sample(state_t, u, wp, sp):
    m = u.shape[0]
    return pl.pallas_call(
        _pool_sample_kernel,
        grid=(1,),
        in_specs=[_const_spec(state_t.shape), _const_spec(u.shape),
                  _const_spec((POOL_GROUPS, POOL_GROUP_DIM, POOL_GROUP_DIM)), _const_spec((1, POOL_WIDTH))],
        out_specs=_const_spec((m, POOL_WIDTH)),
        out_shape=jax.ShapeDtypeStruct((m, POOL_WIDTH), F32),
        compiler_params=_cparams("arbitrary"), name="pool_sample",
    )(state_t, u, wp, sp)


def _flash_t_step(k_blk, q, vt_blk, m_prev, acc_prev, add=None, mask=None):
    s = _dot_nt(k_blk, q)
    if add is not None:
        s = s + add
    if mask is not None:
        s = jnp.where(mask, s, NEG)
    m_new = jnp.maximum(m_prev, jnp.max(s, axis=0, keepdims=True))
    p = jnp.exp2(s - m_new)
    acc = jnp.exp2(m_prev - m_new) * acc_prev + _dot(vt_blk, p.astype(BF16))
    return m_new, acc


def _mla_prompt_kernel(q_ref, k_ref, vt_ref, o_ref, m_scr, acc_scr, *, tk):
    qi = pl.program_id(1)
    tq = q_ref.shape[0]
    per_q = tq // tk
    key_in_tile = lax.broadcasted_iota(jnp.int32, (tk, tq), 0)
    query_in_tile = lax.broadcasted_iota(jnp.int32, (tk, tq), 1)
    m_scr[...] = jnp.full(m_scr.shape, NEG, F32)
    acc_scr[...] = jnp.zeros(acc_scr.shape, F32)

    def block(kb, mask):
        start = pl.multiple_of(kb * tk, tk)
        for hd in range(MLA_HEADS):
            sl = slice(hd * HEAD_PAD, (hd + 1) * HEAD_PAD)
            m_new, acc = _flash_t_step(k_ref[pl.ds(start, tk), sl], q_ref[:, sl], vt_ref[hd, :, pl.ds(start, tk)],
                                       m_scr[hd:hd + 1, :], acc_scr[hd], mask=mask)
            m_scr[hd:hd + 1, :] = m_new
            acc_scr[hd] = acc

    def past(kb, carry):
        block(kb, None)
        return carry

    lax.fori_loop(0, qi * per_q, past, 0)
    for j in range(per_q):
        block(qi * per_q + j, key_in_tile + j * tk <= query_in_tile)
    for hd in range(MLA_HEADS):
        acc = acc_scr[hd]
        o = (acc / acc[MLA_V:MLA_V + 1, :]).T
        o_ref[:, hd * MLA_V:(hd + 1) * MLA_V] = o[:, :MLA_V]


def _mla_prompt(q, k, vt, seq, tq, tk):
    m = q.shape[0]
    nq = seq // tq
    hp = MLA_HEADS * HEAD_PAD
    return pl.pallas_call(
        functools.partial(_mla_prompt_kernel, tk=tk),
        grid=(m // seq, nq),
        in_specs=[
            pl.BlockSpec((tq, hp), lambda b, i: (b * nq + i, 0)),
            pl.BlockSpec((seq, hp), lambda b, i: (b, 0)),
            pl.BlockSpec((MLA_HEADS, HEAD_PAD, seq), lambda b, i: (0, 0, b)),
        ],
        out_specs=pl.BlockSpec((tq, MLA_HEADS * MLA_V), lambda b, i: (b * nq + i, 0)),
        out_shape=jax.ShapeDtypeStruct((m, MLA_HEADS * MLA_V), F32),
        scratch_shapes=[pltpu.VMEM((MLA_HEADS, tq), F32), pltpu.VMEM((MLA_HEADS, HEAD_PAD, tq), F32)],
        compiler_params=_cparams("parallel", "arbitrary"), name="mla_prompt",
    )(q, k, vt)


def _page_copies(pt_ref, b, n_pages, srcs, bufs, sems, token_axes, slot):
    out = []
    for j in range(n_pages):
        pg = pt_ref[b, j]
        for src, buf, sem, axis in zip(srcs, bufs, sems, token_axes):
            window = pl.ds(j * PAGE, PAGE)
            dst = buf.at[slot, window] if axis == 0 else buf.at[slot, :, window]
            out.append(pltpu.make_async_copy(src.at[pg], dst, sem.at[slot]))
    return out


def _paged_prefetch(pt_ref, n_pages, srcs, bufs, sems, token_axes):
    b = pl.program_id(0)
    nb = pl.num_programs(0)
    slot = b % 2
    copies = functools.partial(_page_copies, pt_ref, n_pages=n_pages, srcs=srcs, bufs=bufs, sems=sems,
                               token_axes=token_axes)

    @pl.when(b == 0)
    def _():
        for c in copies(b=0, slot=0):
            c.start()

    @pl.when(b + 1 < nb)
    def _():
        for c in copies(b=b + 1, slot=1 - slot):
            c.start()

    for c in copies(b=b, slot=slot):
        c.wait()
    return slot


def _mla_decode_kernel(pt_ref, q_ref, qabs_ref, knew_ref, cnew_ref, gkr_ref, wukt_ref, cos_ref, sin_ref,
                       ckv_hbm, kr_hbm, o_ref, ckv_buf, kr_buf, s_scr, sem_c, sem_r, *, n_pages, chunk):
    t_past = n_pages * PAGE
    slot = _paged_prefetch(pt_ref, n_pages, (ckv_hbm, kr_hbm), (ckv_buf, kr_buf), (sem_c, sem_r), (0, 1))

    q = q_ref[0].astype(F32)
    s_new = jnp.sum(q * knew_ref[0].astype(F32), axis=-1, keepdims=True)
    lane = lax.broadcasted_iota(jnp.int32, (MLA_HEADS, HEAD_PAD), 1)
    first = (lane >= MLA_NOPE) & (lane < MLA_NOPE + 16)
    second = (lane >= MLA_NOPE + 16) & (lane < MLA_QK)
    psi2 = jnp.where(first, pltpu.roll(q, HEAD_PAD - 16, 1), 0.0) - jnp.where(second, pltpu.roll(q, 16, 1), 0.0)
    gkr = gkr_ref[...]
    psi1 = (q * gkr)[:, MLA_NOPE:MLA_QK].astype(BF16)
    psi2 = (psi2 * gkr)[:, MLA_NOPE:MLA_QK].astype(BF16)
    lhs = jnp.concatenate([wukt_ref[...], qabs_ref[0]], axis=0)
    n_up = MLA_HEADS * MLA_NOPE

    def score_chunk(c, _):
        start = pl.multiple_of(c * chunk, chunk)
        ckv_c = ckv_buf[slot, pl.ds(start, chunk), :].astype(BF16)
        r = _dot_nt(lhs, ckv_c)
        kn = r[:n_up]
        k2 = jnp.sum((kn * kn).reshape(MLA_HEADS, MLA_NOPE, chunk), axis=1)
        kr_c = kr_buf[slot, :, pl.ds(start, chunk)]
        cos_c = cos_ref[:, pl.ds(start, chunk)]
        sin_c = sin_ref[:, pl.ds(start, chunk)]
        s_rope = _dot(psi1, (kr_c * cos_c).astype(BF16)) + _dot(psi2, (kr_c * sin_c).astype(BF16))
        kr2 = jnp.sum(kr_c * kr_c, axis=0, keepdims=True)
        inv = lax.rsqrt((k2 + kr2) * (1.0 / MLA_QK) + EPS)
        s_scr[:, pl.ds(start, chunk)] = (r[n_up:n_up + MLA_HEADS] + s_rope) * inv
        return 0

    lax.fori_loop(0, t_past // chunk, score_chunk, 0)
    s = s_scr[...]
    m = jnp.maximum(jnp.max(s, axis=-1, keepdims=True), s_new)
    p = jnp.exp2(s - m)
    p_new = jnp.exp2(s_new - m)
    denom = jnp.sum(p, axis=-1, keepdims=True) + p_new
    s_scr[...] = p

    def pv_chunk(c, acc):
        start = pl.multiple_of(c * chunk, chunk)
        ckv_c = ckv_buf[slot, pl.ds(start, chunk), :].astype(BF16)
        return acc + _dot(s_scr[:, pl.ds(start, chunk)].astype(BF16), ckv_c)

    acc = lax.fori_loop(0, t_past // chunk, pv_chunk, jnp.zeros((MLA_HEADS, MLA_KV_LORA), F32))
    o_ref[0] = (acc + p_new * cnew_ref[0]) / denom


def _mla_decode(page_table, q, qabs16, knew, cnew, gkr, wukt2, cos32, sin32, cache_ckv, cache_kr, chunk):
    nb, n_pages = page_table.shape
    t_past = n_pages * PAGE
    kern = functools.partial(_mla_decode_kernel, n_pages=n_pages, chunk=chunk)
    b3 = lambda b, pt: (b, 0, 0)
    grid_spec = pltpu.PrefetchScalarGridSpec(
        num_scalar_prefetch=1, grid=(nb,),
        in_specs=[
            pl.BlockSpec((1, MLA_HEADS, HEAD_PAD), b3), pl.BlockSpec((1, 16, MLA_KV_LORA), b3),
            pl.BlockSpec((1, MLA_HEADS, HEAD_PAD), b3), pl.BlockSpec((1, 1, MLA_KV_LORA), b3),
            _const_spec((1, HEAD_PAD)), _const_spec((MLA_HEADS * MLA_NOPE, MLA_KV_LORA)),
            _const_spec((MLA_ROPE, t_past)), _const_spec((MLA_ROPE, t_past)),
            pl.BlockSpec(memory_space=pl.ANY), pl.BlockSpec(memory_space=pl.ANY),
        ],
        out_specs=pl.BlockSpec((1, MLA_HEADS, MLA_KV_LORA), b3),
        scratch_shapes=[
            pltpu.VMEM((2, t_past, MLA_KV_LORA), F32), pltpu.VMEM((2, MLA_ROPE, t_past), F32),
            pltpu.VMEM((MLA_HEADS, t_past), F32),
            pltpu.SemaphoreType.DMA((2,)), pltpu.SemaphoreType.DMA((2,)),
        ],
    )
    return pl.pallas_call(
        kern, grid_spec=grid_spec,
        out_shape=jax.ShapeDtypeStruct((nb, MLA_HEADS, MLA_KV_LORA), F32),
        compiler_params=_cparams("arbitrary"), name="mla_decode",
    )(page_table, q, qabs16, knew, cnew, gkr, wukt2, cos32, sin32, cache_ckv, cache_kr)


def _out_ffn_kernel(*refs, n_parts, latent):
    parts = refs[:n_parts]
    x_ref = refs[n_parts]
    idx = n_parts + 1
    if latent:
        wuv_ref = refs[idx]
        idx += 1
    wouts = refs[idx:idx + n_parts]
    gffn_ref, wgu_ref, wd_ref, o_ref, h_scr = refs[idx + n_parts:]
    mix = jnp.zeros(x_ref.shape, F32)
    for p_ref, w_ref in zip(parts, wouts):
        if latent and p_ref is parts[-1]:
            for hd in range(MLA_HEADS):
                lat = p_ref[:, hd * MLA_KV_LORA:(hd + 1) * MLA_KV_LORA].astype(BF16)
                a = _dot(lat, wuv_ref[hd]).astype(BF16)
                mix = mix + _dot(a, w_ref[hd * MLA_V:(hd + 1) * MLA_V, :])
        else:
            mix = mix + _dot(p_ref[...].astype(BF16), w_ref[...])
    x1 = x_ref[...] + mix
    h_scr[...] = _rms(x1, gffn_ref[...]).astype(BF16)
    acc = jnp.zeros_like(x1)
    for c in range(FFN_HIDDEN // FFN_CHUNK):
        lo = c * FFN_CHUNK
        gate = _dot(h_scr[...], wgu_ref[:, lo:lo + FFN_CHUNK])
        up = _dot(h_scr[...], wgu_ref[:, FFN_HIDDEN + lo:FFN_HIDDEN + lo + FFN_CHUNK])
        act = (gate / (1.0 + jnp.exp(-gate)) * up).astype(BF16)
        acc = acc + _dot(act, wd_ref[lo:lo + FFN_CHUNK, :])
    o_ref[...] = x1 + acc


def _out_ffn(parts, x, wouts, gffn, wgu, wd, tm, wuv=None):
    m = x.shape[0]
    row = lambda i: (i, 0)
    in_specs = [pl.BlockSpec((tm, p.shape[1]), row) for p in parts] + [pl.BlockSpec((tm, D_MODEL), row)]
    args = list(parts) + [x]
    if wuv is not None:
        in_specs.append(_const_spec(wuv.shape))
        args.append(wuv)
    in_specs += [_const_spec(w.shape) for w in wouts]
    in_specs += [_const_spec((1, D_MODEL)), _const_spec(wgu.shape), _const_spec(wd.shape)]
    args += list(wouts) + [gffn, wgu, wd]
    return pl.pallas_call(
        functools.partial(_out_ffn_kernel, n_parts=len(parts), latent=wuv is not None),
        grid=(m // tm,), in_specs=in_specs,
        out_specs=pl.BlockSpec((tm, D_MODEL), row),
        out_shape=jax.ShapeDtypeStruct((m, D_MODEL), F32),
        scratch_shapes=[pltpu.VMEM((tm, D_MODEL), BF16)],
        compiler_params=_cparams("parallel"), name="out_ffn",
    )(*args)


def _prep_c_kernel(x_ref, gmix_ref, wqkv_ref, gq_ref, gk_ref, seg_ref, ca_ref, cb1_ref, cb2_ref,
                   kf_ref, vf_ref, q_ref, k_ref, vt_ref):
    h = _rms(x_ref[...], gmix_ref[...]).astype(BF16)
    qkv = _dot(h, wqkv_ref[...])
    ca = ca_ref[...]
    cb1 = cb1_ref[...]
    cb2 = cb2_ref[...]
    seg = seg_ref[...]

    def norm_rope(t, g):
        ms = _dot((t * t).astype(BF16), seg) * (1.0 / C_HEAD_DIM)
        t = t * lax.rsqrt(ms + EPS) * g
        return t * ca + pltpu.roll(t, LANES - C_ROT // 2, 1) * cb1 + pltpu.roll(t, C_ROT // 2, 1) * cb2

    lane = lax.broadcasted_iota(jnp.int32, (1, LANES), 1)
    low = lane < C_HEAD_DIM
    for j in range(C_Q_WIDTH // LANES):
        qt = norm_rope(qkv[:, j * LANES:(j + 1) * LANES], gq_ref[...])
        q_ref[j] = (qt * (C_SCALE * LOG2E)).astype(BF16)
    for j in range(C_KV_WIDTH // LANES):
        kt = norm_rope(qkv[:, C_Q_WIDTH + j * LANES:C_Q_WIDTH + (j + 1) * LANES], gk_ref[...])
        vt = qkv[:, C_Q_WIDTH + C_KV_WIDTH + j * LANES:C_Q_WIDTH + C_KV_WIDTH + (j + 1) * LANES]
        kf_ref[:, j * LANES:(j + 1) * LANES] = kt
        vf_ref[:, j * LANES:(j + 1) * LANES] = vt
        k_ref[2 * j] = jnp.where(low, kt, 0.0).astype(BF16)
        k_ref[2 * j + 1] = jnp.where(low, 0.0, kt).astype(BF16)
        vt_ref[2 * j] = jnp.where(low, vt, 1.0).T.astype(BF16)
        vt_ref[2 * j + 1] = jnp.where(low, 1.0, vt).T.astype(BF16)


def _prep_c(x, gmix, wc, tabs, tm):
    m = x.shape[0]
    n_tab = tabs[0].shape[0] // tm
    row = lambda i: (i, 0)
    tab = lambda i: (i % n_tab, 0)
    nq = C_Q_WIDTH // LANES
    return pl.pallas_call(
        _prep_c_kernel,
        grid=(m // tm,),
        in_specs=[
            pl.BlockSpec((tm, D_MODEL), row), _const_spec((1, D_MODEL)),
            _const_spec((D_MODEL, C_Q_WIDTH + 2 * C_KV_WIDTH)),
            _const_spec((1, LANES)), _const_spec((1, LANES)), _const_spec((LANES, LANES)),
            pl.BlockSpec((tm, LANES), tab), pl.BlockSpec((tm, LANES), tab), pl.BlockSpec((tm, LANES), tab),
        ],
        out_specs=[
            pl.BlockSpec((tm, C_KV_WIDTH), row), pl.BlockSpec((tm, C_KV_WIDTH), row),
            pl.BlockSpec((nq, tm, LANES), lambda i: (0, i, 0)),
            pl.BlockSpec((C_KV_HEADS, tm, LANES), lambda i: (0, i, 0)),
            pl.BlockSpec((C_KV_HEADS, LANES, tm), lambda i: (0, 0, i)),
        ],
        out_shape=[
            jax.ShapeDtypeStruct((m, C_KV_WIDTH), F32), jax.ShapeDtypeStruct((m, C_KV_WIDTH), F32),
            jax.ShapeDtypeStruct((nq, m, LANES), BF16),
            jax.ShapeDtypeStruct((C_KV_HEADS, m, LANES), BF16),
            jax.ShapeDtypeStruct((C_KV_HEADS, LANES, m), BF16),
        ],
        compiler_params=_cparams("parallel"), name="prep_c",
    )(x, gmix, wc["w_qkv"], wc["g_q"], wc["g_k"], wc["seg"], *tabs)


def _topk_rank_mask(g, valid, n_rows):
    row = lax.broadcasted_iota(jnp.int32, g.shape, 0)
    g = jnp.where(valid, g, NEG)
    cnt = jnp.zeros(g.shape, F32)
    for mrow in range(n_rows):
        gm = g[mrow:mrow + 1, :]
        cnt = cnt + jnp.where(row > mrow, jnp.where(gm >= g, 1.0, 0.0), jnp.where(gm > g, 1.0, 0.0))
    return jnp.where(valid & (cnt < MOBA_TOPK), 1.0, 0.0)


def _moba_prompt_kernel(q_ref, k_ref, vt_ref, o_ref, m_scr, acc_scr, bias_scr, *, n_blk):
    qi = pl.program_id(1)
    blk = MOBA_BLOCK
    nq = C_GROUP * blk
    causal = (lax.broadcasted_iota(jnp.int32, (blk, nq), 0)
              <= (lax.broadcasted_iota(jnp.int32, (blk, nq), 1) & (blk - 1)))
    cand = lax.broadcasted_iota(jnp.int32, (8, nq), 0)
    cand_k = lax.broadcasted_iota(jnp.int32, (8, LANES), 0)

    def queries(kv):
        pair = kv // 2
        return q_ref[pair * C_GROUP:(pair + 1) * C_GROUP].reshape(nq, LANES)

    for kv in range(C_KV_HEADS):
        kmean = jnp.zeros((8, LANES), F32)
        for n in range(n_blk):
            ksum = jnp.sum(k_ref[kv, n * blk:(n + 1) * blk, :].astype(F32), axis=0, keepdims=True)
            kmean = jnp.where(cand_k == n, ksum * (1.0 / blk), kmean)
        k_hi = kmean.astype(BF16)
        k_lo = (kmean - k_hi.astype(F32)).astype(BF16)
        q = queries(kv)
        gate_t = _dot_nt(k_hi, q) + _dot_nt(k_lo, q)
        bias_scr[kv] = (_topk_rank_mask(gate_t, cand < qi, 8) - 1.0) * (-NEG)
    m_scr[...] = jnp.full(m_scr.shape, NEG, F32)
    acc_scr[...] = jnp.zeros(acc_scr.shape, F32)

    def block(keys, add_row, mask):
        for kv in range(C_KV_HEADS):
            add = None if add_row is None else bias_scr[kv, add_row:add_row + 1, :]
            m_new, acc = _flash_t_step(k_ref[kv, keys, :], queries(kv), vt_ref[kv, :, keys],
                                       m_scr[kv:kv + 1, :], acc_scr[kv], add=add, mask=mask)
            m_scr[kv:kv + 1, :] = m_new
            acc_scr[kv] = acc

    for n in range(n_blk - 1):
        @pl.when(n < qi)
        def _():
            block(slice(n * blk, (n + 1) * blk), n, None)

    block(pl.ds(pl.multiple_of(qi * blk, blk), blk), None, causal)

    low = lax.broadcasted_iota(jnp.int32, (LANES, nq), 0) < C_HEAD_DIM
    for pair in range(C_KV_HEADS // 2):
        even = acc_scr[2 * pair]
        odd = acc_scr[2 * pair + 1]
        o_t = jnp.where(low, even / even[C_HEAD_DIM:C_HEAD_DIM + 1, :], odd / odd[0:1, :])
        for g in range(C_GROUP):
            tile = pair * C_GROUP + g
            o_ref[:, tile * LANES:(tile + 1) * LANES] = o_t[:, g * blk:(g + 1) * blk].T.astype(BF16)


def _moba_prompt(q, k, vt, seq):
    nq_t, m, _ = q.shape
    n_blk = seq // MOBA_BLOCK
    assert n_blk <= 8
    nq = C_GROUP * MOBA_BLOCK
    return pl.pallas_call(
        functools.partial(_moba_prompt_kernel, n_blk=n_blk),
        grid=(m // seq, n_blk),
        in_specs=[
            pl.BlockSpec((nq_t, MOBA_BLOCK, LANES), lambda b, i: (0, b * n_blk + i, 0)),
            pl.BlockSpec((C_KV_HEADS, seq, LANES), lambda b, i: (0, b, 0)),
            pl.BlockSpec((C_KV_HEADS, LANES, seq), lambda b, i: (0, 0, b)),
        ],
        out_specs=pl.BlockSpec((MOBA_BLOCK, nq_t * LANES), lambda b, i: (b * n_blk + i, 0)),
        out_shape=jax.ShapeDtypeStruct((m, nq_t * LANES), BF16),
        scratch_shapes=[pltpu.VMEM((C_KV_HEADS, nq), F32), pltpu.VMEM((C_KV_HEADS, LANES, nq), F32),
                        pltpu.VMEM((C_KV_HEADS, 8, nq), F32)],
        compiler_params=_cparams("parallel", "arbitrary"), name="moba_prompt",
    )(q, k, vt)


def _moba_decode_kernel(pt_ref, qbd_ref, knew_ref, vnew_ref, k_hbm, v_hbm, o_ref,
                        k_buf, v_buf, s_scr, sem_k, sem_v, *, n_pages, chunk):
    t_past = n_pages * PAGE
    n_blk = t_past // MOBA_BLOCK
    slot = _paged_prefetch(pt_ref, n_pages, (k_hbm, v_hbm), (k_buf, v_buf), (sem_k, sem_v), (1, 1))
    qbd = qbd_ref[0]
    s_new = jnp.sum(qbd.astype(F32) * knew_ref[0], axis=-1, keepdims=True)

    def score_chunk(c, _):
        start = pl.multiple_of(c * chunk, chunk)
        s_scr[:, pl.ds(start, chunk)] = _dot(qbd, k_buf[slot, :, pl.ds(start, chunk)].astype(BF16))
        return 0

    lax.fori_loop(0, t_past // chunk, score_chunk, 0)

    gates, maxes = [], []
    for n in range(n_blk):
        s_n = s_scr[:, n * MOBA_BLOCK:(n + 1) * MOBA_BLOCK]
        gates.append(jnp.sum(s_n, axis=-1, keepdims=True))
        maxes.append(jnp.max(s_n, axis=-1, keepdims=True))
    sels = []
    for n in range(n_blk):
        cnt = jnp.zeros_like(gates[n])
        for mb in range(n_blk):
            if mb != n:
                beats = (gates[mb] >= gates[n]) if mb < n else (gates[mb] > gates[n])
                cnt = cnt + jnp.where(beats, 1.0, 0.0)
        sels.append(cnt < MOBA_TOPK)
    m = s_new
    for n in range(n_blk):
        m = jnp.maximum(m, jnp.where(sels[n], maxes[n], NEG))
    p_new = jnp.exp2(s_new - m)
    psum = jnp.zeros((C_HEADS, MOBA_BLOCK), F32)
    for n in range(n_blk):
        sl = slice(n * MOBA_BLOCK, (n + 1) * MOBA_BLOCK)
        p = jnp.where(sels[n], jnp.exp2(s_scr[:, sl] - m), 0.0)
        s_scr[:, sl] = p
        psum = psum + p
    denom = jnp.sum(psum, axis=-1, keepdims=True) + p_new

    def pv_chunk(c, acc):
        start = pl.multiple_of(c * chunk, chunk)
        return acc + _dot_nt(s_scr[:, pl.ds(start, chunk)].astype(BF16),
                             v_buf[slot, :, pl.ds(start, chunk)].astype(BF16))

    acc = lax.fori_loop(0, t_past // chunk, pv_chunk, jnp.zeros((C_HEADS, C_KV_WIDTH), F32))
    o_ref[0] = (acc + p_new * vnew_ref[0]) / denom


def _moba_decode(page_table, qbd, knew, vnew, cache_k, cache_v, chunk):
    nb, n_pages = page_table.shape
    t_past = n_pages * PAGE
    b3 = lambda b, pt: (b, 0, 0)
    grid_spec = pltpu.PrefetchScalarGridSpec(
        num_scalar_prefetch=1, grid=(nb,),
        in_specs=[
            pl.BlockSpec((1, C_HEADS, C_KV_WIDTH), b3), pl.BlockSpec((1, 1, C_KV_WIDTH), b3),
            pl.BlockSpec((1, 1, C_KV_WIDTH), b3),
            pl.BlockSpec(memory_space=pl.ANY), pl.BlockSpec(memory_space=pl.ANY),
        ],
        out_specs=pl.BlockSpec((1, C_HEADS, C_KV_WIDTH), b3),
        scratch_shapes=[
            pltpu.VMEM((2, C_KV_WIDTH, t_past), F32), pltpu.VMEM((2, C_KV_WIDTH, t_past), F32),
            pltpu.VMEM((C_HEADS, t_past), F32),
            pltpu.SemaphoreType.DMA((2,)), pltpu.SemaphoreType.DMA((2,)),
        ],
    )
    return pl.pallas_call(
        functools.partial(_moba_decode_kernel, n_pages=n_pages, chunk=chunk), grid_spec=grid_spec,
        out_shape=jax.ShapeDtypeStruct((nb, C_HEADS, C_KV_WIDTH), F32),
        compiler_params=_cparams("arbitrary"), name="moba_decode",
    )(page_table, qbd, knew, vnew, cache_k, cache_v)


def _rope_angles(pos, dim, theta):
    inv_freq = jnp.exp(jnp.arange(0, dim, 2, dtype=F32) * (-math.log(theta) / dim))
    ang = pos.astype(F32)[:, None] * inv_freq[None, :]
    return jnp.cos(ang), jnp.sin(ang)


def _mla_tables(pos):
    cos, sin = _rope_angles(pos, MLA_ROPE, MLA_THETA)
    z = lambda n: jnp.zeros((pos.shape[0], n), F32)
    half = MLA_ROPE // 2
    tail = HEAD_PAD - MLA_QK
    cos_t = jnp.concatenate([z(MLA_NOPE), cos, cos, z(tail)], axis=1)
    sina = jnp.concatenate([z(MLA_NOPE), -sin, z(half), z(tail)], axis=1)
    sinb = jnp.concatenate([z(MLA_NOPE), z(half), sin, z(tail)], axis=1)
    return cos_t, sina, sinb


def _moba_tables(pos):
    cos, sin = _rope_angles(pos, C_ROT, C_THETA)
    n = pos.shape[0]
    half = C_ROT // 2
    rest = C_HEAD_DIM - C_ROT
    one = jnp.ones((n, rest), F32)
    z = lambda k: jnp.zeros((n, k), F32)
    ca = jnp.concatenate([cos, cos, one] * 2, axis=1)
    cb1 = jnp.concatenate([-sin, z(half), z(rest)] * 2, axis=1)
    cb2 = jnp.concatenate([z(half), sin, z(rest)] * 2, axis=1)
    return ca, cb1, cb2


def _pad_lanes(a, start, total):
    pad = [(0, 0)] * (a.ndim - 1) + [(start, total - start - a.shape[-1])]
    return jnp.pad(a, pad)


def _layer_a_weights(w_in, g_q_lora, g_kv_lora, w_uq, w_uk, w_uv, g_q, g_k, w_pool, s_pool, w_out):
    o_kr = POOL_WIDTH + MLA_Q_LORA + MLA_KV_LORA
    w_in_p = jnp.concatenate([w_in[:, :o_kr], _pad_lanes(w_in[:, o_kr:], MLA_NOPE, HEAD_PAD)], axis=1)
    hp = MLA_HEADS * HEAD_PAD
    return {
        "w_in": w_in_p.astype(BF16),
        "g_q_lora": g_q_lora[None], "g_kv_lora": g_kv_lora[None],
        "w_uq": _pad_lanes(w_uq, 0, HEAD_PAD).reshape(MLA_Q_LORA, hp).astype(BF16),
        "w_uk": _pad_lanes(w_uk, 0, HEAD_PAD).reshape(MLA_KV_LORA, hp).astype(BF16),
        "w_uv": _pad_lanes(w_uv, 0, HEAD_PAD).reshape(MLA_KV_LORA, hp).astype(BF16),
        "w_uk_t": _pad_lanes(jnp.transpose(w_uk, (1, 0, 2)), 0, HEAD_PAD).transpose(0, 2, 1).astype(BF16),
        "w_uk_t2": jnp.transpose(w_uk, (1, 2, 0)).reshape(MLA_HEADS * MLA_NOPE, MLA_KV_LORA).astype(BF16),
        "w_uv_h": jnp.transpose(w_uv, (1, 0, 2)).astype(BF16),
        "g_q": _pad_lanes(g_q, 0, HEAD_PAD)[None],
        "g_kn": _pad_lanes(g_k[:MLA_NOPE], 0, HEAD_PAD)[None],
        "g_kr": _pad_lanes(g_k[MLA_NOPE:], MLA_NOPE, HEAD_PAD)[None],
        "w_pool": w_pool.astype(BF16), "s_pool": s_pool[None],
        "w_out_pool": w_out[:POOL_WIDTH].astype(BF16), "w_out_mla": w_out[POOL_WIDTH:].astype(BF16),
    }


def _moba_tile_heads():
    order = []
    for j in range(C_Q_WIDTH // LANES):
        p, g = divmod(j, C_GROUP)
        order += [(2 * p) * C_GROUP + g, (2 * p + 1) * C_GROUP + g]
    return order


def _layer_c_weights(w_qkv, g_q, g_k, w_o):
    order = jnp.array(_moba_tile_heads())
    wq = w_qkv[:, :C_Q_WIDTH].reshape(D_MODEL, C_HEADS, C_HEAD_DIM)[:, order].reshape(D_MODEL, C_Q_WIDTH)
    w_o_p = w_o.reshape(C_HEADS, C_HEAD_DIM, D_MODEL)[order].reshape(C_Q_WIDTH, D_MODEL)
    seg = jnp.kron(jnp.eye(LANES // C_HEAD_DIM, dtype=F32), jnp.ones((C_HEAD_DIM, C_HEAD_DIM), F32))
    return {
        "w_qkv": jnp.concatenate([wq, w_qkv[:, C_Q_WIDTH:]], axis=1).astype(BF16),
        "g_q": jnp.tile(g_q, 2)[None], "g_k": jnp.tile(g_k, 2)[None], "seg": seg.astype(BF16),
        "w_o": w_o_p.astype(BF16),
    }


def _pages_feature_major(cache):
    n = cache.shape[0]
    return jnp.transpose(cache, (0, 2, 3, 1)).reshape(n, C_KV_WIDTH, PAGE)


def _tiles_to_rows(t):
    return jnp.transpose(t, (1, 0, 2)).reshape(t.shape[1], t.shape[0] * LANES)


def _pick_tile(m, pref):
    return pref if m % pref == 0 else m


def kernel(x_prompt, x_sample, cache_mla_ckv, cache_mla_krope, state_pool, cache_moba_k, cache_moba_v,
           page_table, g_mix, g_ffn, w_in_a, g_q_lora, g_kv_lora, w_uq, w_uk, w_uv, g_mla_q, g_mla_k,
           w_pool, s_pool, w_out_a, w_qkv_c, g_moba_q, g_moba_k, w_o_c, w_gate_up, w_down):
    bp, seq, _ = x_prompt.shape
    bs = x_sample.shape[0]
    n_pages = page_table.shape[1]
    past = n_pages * PAGE
    depth = g_mix.shape[0]
    mp = bp * seq
    xp = x_prompt.reshape(mp, D_MODEL)
    xs = x_sample.reshape(bs, D_MODEL)
    tm_p = _pick_tile(mp, 256)
    tm_f = _pick_tile(mp, 512)
    tq = next(t for t in MLA_TILES if seq % t == 0)
    chunk = min(1024, past)

    pos_p = jnp.arange(seq, dtype=jnp.int32)
    pos_s = jnp.full((bs,), past, dtype=jnp.int32)
    wgu = w_gate_up.astype(BF16)
    wdn = w_down.astype(BF16)

    outs_p = {k: [] for k in ("ckv", "kr", "pool", "k", "v")}
    outs_s = {k: [] for k in ("ckv", "kr", "pool", "k", "v")}
    for layer in range(depth):
        i = layer // 2
        gmix = g_mix[layer][None]
        gffn = g_ffn[layer][None]
        if layer % 2 == 0:
            wa = _layer_a_weights(w_in_a[i], g_q_lora[i], g_kv_lora[i], w_uq[i], w_uk[i], w_uv[i],
                                  g_mla_q[i], g_mla_k[i], w_pool[i], s_pool[i], w_out_a[i])
            u, ckv, kr, q, k, v = _prep_a(xp, gmix, wa, _mla_tables(pos_p), tm_p, decode=False)
            pool_out = _pool_prompt(u, wa["w_pool"], wa["s_pool"], seq, tm_p)
            mla_out = _mla_prompt(q, k, v, seq, tq, tq)
            xp = _out_ffn([pool_out, mla_out], xp, [wa["w_out_pool"], wa["w_out_mla"]], gffn,
                          wgu[layer], wdn[layer], tm_f)
            outs_p["ckv"].append(ckv.reshape(bp, seq, MLA_KV_LORA))
            outs_p["kr"].append(kr.reshape(bp, seq, MLA_ROPE))
            outs_p["pool"].append(u.reshape(bp, seq, POOL_WIDTH)[:, seq - POOL_STATE:])
            u, ckv, kr, q, k, v, qabs = _prep_a(xs, gmix, wa, _mla_tables(pos_s), bs, decode=True)
            state = state_pool[i]
            pool_out = _pool_sample(jnp.transpose(state, (1, 0, 2)), u, wa["w_pool"], wa["s_pool"])
            cos_k, sin_k = _rope_angles(jnp.arange(past, dtype=jnp.int32), MLA_ROPE, MLA_THETA)
            qabs16 = jnp.pad(qabs.reshape(bs, MLA_HEADS, MLA_KV_LORA), ((0, 0), (0, 16 - MLA_HEADS), (0, 0)))
            o_lat = _mla_decode(
                page_table, q.reshape(bs, MLA_HEADS, HEAD_PAD), qabs16, k.reshape(bs, MLA_HEADS, HEAD_PAD),
                ckv.reshape(bs, 1, MLA_KV_LORA), wa["g_kr"], wa["w_uk_t2"],
                jnp.concatenate([cos_k, cos_k], axis=1).T, jnp.concatenate([sin_k, sin_k], axis=1).T,
                cache_mla_ckv[i], jnp.transpose(cache_mla_krope[i], (0, 2, 1)), chunk)
            xs = _out_ffn([pool_out, o_lat.reshape(bs, MLA_HEADS * MLA_KV_LORA)], xs,
                          [wa["w_out_pool"], wa["w_out_mla"]], gffn, wgu[layer], wdn[layer], bs,
                          wuv=wa["w_uv_h"])
            outs_s["ckv"].append(ckv.reshape(bs, 1, MLA_KV_LORA))
            outs_s["kr"].append(kr.reshape(bs, 1, MLA_ROPE))
            outs_s["pool"].append(jnp.concatenate([state[:, 1:], u[:, None]], axis=1))
        else:
            wc = _layer_c_weights(w_qkv_c[i], g_moba_q[i], g_moba_k[i], w_o_c[i])
            kf, vf, q, k, v = _prep_c(xp, gmix, wc, _moba_tables(pos_p), tm_p)
            o = _moba_prompt(q, k, v, seq)
            xp = _out_ffn([o], xp, [wc["w_o"]], gffn, wgu[layer], wdn[layer], tm_f)
            outs_p["k"].append(kf.reshape(bp, seq, C_KV_HEADS, C_HEAD_DIM))
            outs_p["v"].append(vf.reshape(bp, seq, C_KV_HEADS, C_HEAD_DIM))
            kf, vf, q, k, v = _prep_c(xs, gmix, wc, _moba_tables(pos_s), bs)
            qh = _tiles_to_rows(q).reshape(bs, C_Q_WIDTH // LANES, 2, C_HEAD_DIM)
            kv_of = jnp.array(_moba_tile_heads()).reshape(-1, 2) // C_GROUP
            onehot = (kv_of[:, :, None] == jnp.arange(C_KV_HEADS)[None, None, :]).astype(BF16)
            qbd = (qh[:, :, :, None, :] * onehot[None, :, :, :, None]).reshape(bs, C_HEADS, C_KV_WIDTH)
            o = _moba_decode(page_table, qbd, kf.reshape(bs, 1, C_KV_WIDTH), vf.reshape(bs, 1, C_KV_WIDTH),
                             _pages_feature_major(cache_moba_k[i]), _pages_feature_major(cache_moba_v[i]), chunk)
            o = (o.reshape(bs, C_Q_WIDTH // LANES, 2, C_KV_HEADS, C_HEAD_DIM)
                 * onehot[None, :, :, :, None].astype(F32)).sum(axis=3).reshape(bs, C_Q_WIDTH)
            xs = _out_ffn([o], xs, [wc["w_o"]], gffn, wgu[layer], wdn[layer], bs)
            outs_s["k"].append(kf.reshape(bs, 1, C_KV_HEADS, C_HEAD_DIM))
            outs_s["v"].append(vf.reshape(bs, 1, C_KV_HEADS, C_HEAD_DIM))

    st = lambda xs_: jnp.stack(xs_)
    return (xp.reshape(bp, seq, D_MODEL), xs.reshape(bs, 1, D_MODEL),
            st(outs_p["ckv"]), st(outs_p["kr"]), st(outs_p["pool"]), st(outs_p["k"]), st(outs_p["v"]),
            st(outs_s["ckv"]), st(outs_s["kr"]), st(outs_s["pool"]), st(outs_s["k"]), st(outs_s["v"]))
```

```python
import functools
import math

import jax
import jax.numpy as jnp
from jax import lax
from jax.experimental import pallas as pl
from jax.experimental.pallas import tpu as pltpu

F32 = jnp.float32
BF16 = jnp.bfloat16
LOG2E = math.log2(math.e)

D_MODEL = 1024
EPS = 1e-6
NEG = -1e30

POOL_WIDTH = 512
POOL_GROUPS = 4
POOL_GROUP_DIM = 128
POOL_WINDOWS = (2, 4, 8, 16)
POOL_STATE = 15
POOL_HALO = 16

MLA_HEADS = 8
MLA_NOPE = 64
MLA_ROPE = 32
MLA_QK = 96
MLA_V = 64
MLA_Q_LORA = 384
MLA_KV_LORA = 256
MLA_THETA = 10000.0
MLA_SCALE = MLA_QK ** -0.5
HEAD_PAD = 128
MLA_TILES = (1024, 512, 256)
A_IN_PAD = POOL_WIDTH + MLA_Q_LORA + MLA_KV_LORA + HEAD_PAD

C_HEADS = 16
C_KV_HEADS = 4
C_GROUP = 4
C_HEAD_DIM = 64
C_ROT = 16
C_THETA = 500000.0
C_SCALE = C_HEAD_DIM ** -0.5
MOBA_BLOCK = 256
MOBA_TOPK = 3
C_Q_WIDTH = C_HEADS * C_HEAD_DIM
C_KV_WIDTH = C_KV_HEADS * C_HEAD_DIM

FFN_HIDDEN = 2816
FFN_CHUNK = 256

PAGE = 128
LANES = 128
TOKEN_TILES = (1024, 512, 256)
MLA_DECODE_CHUNK = 4096
MOBA_DECODE_CHUNK = 8192
VMEM_LIMIT = 56 * 1024 * 1024


def _cparams(*sem):
    return pltpu.CompilerParams(dimension_semantics=sem, vmem_limit_bytes=VMEM_LIMIT)


def _const_spec(shape):
    nd = len(shape)
    return pl.BlockSpec(shape, lambda *_: (0,) * nd, pipeline_mode=pl.Buffered(1))


def _rms(x, g):
    ms = jnp.mean(x * x, axis=-1, keepdims=True)
    return x * lax.rsqrt(ms + EPS) * g


def _dot(a, b):
    return jnp.dot(a, b, preferred_element_type=F32)


def _dot_nt(a, b):
    return lax.dot_general(a, b, (((1,), (1,)), ((), ())), preferred_element_type=F32)


def _prep_a_kernel(x_ref, gmix_ref, win_ref, gql_ref, gkvl_ref, wuq_ref, gq_ref, gkn_ref, gkr_ref,
                   wuk_ref, wuv_ref, cos_ref, sina_ref, sinb_ref, *rest, decode):
    if decode:
        wukt_ref, u_ref, ckv_ref, kr_ref, q_ref, k_ref, vt_ref, qabs_ref = rest
    else:
        u_ref, ckv_ref, kr_ref, q_ref, k_ref, vt_ref = rest
    h = _rms(x_ref[...], gmix_ref[...]).astype(BF16)
    hw = _dot(h, win_ref[...])
    u_ref[...] = hw[:, :POOL_WIDTH]
    o_q = POOL_WIDTH
    o_kv = o_q + MLA_Q_LORA
    o_kr = o_kv + MLA_KV_LORA
    cq = _rms(hw[:, o_q:o_kv], gql_ref[...]).astype(BF16)
    ckv = _rms(hw[:, o_kv:o_kr], gkvl_ref[...])
    ckv_ref[...] = ckv
    kr = hw[:, o_kr:o_kr + HEAD_PAD]
    kr_ref[...] = kr[:, MLA_NOPE:MLA_QK]
    cos = cos_ref[...]
    sina = sina_ref[...]
    sinb = sinb_ref[...]

    def rope(t, base):
        return t * base + pltpu.roll(t, HEAD_PAD - 16, 1) * sina + pltpu.roll(t, 16, 1) * sinb

    ones = jnp.ones((HEAD_PAD, HEAD_PAD), BF16)

    def sum_sq(t):
        return _dot((t * t).astype(BF16), ones)

    q = _dot(cq, wuq_ref[...])
    gq = gq_ref[...]
    qbase = gkn_ref[...] + cos
    for hd in range(MLA_HEADS):
        sl = slice(hd * HEAD_PAD, (hd + 1) * HEAD_PAD)
        blk = q[:, sl]
        ms = sum_sq(blk) * (1.0 / MLA_QK)
        qt = (rope(blk * lax.rsqrt(ms + EPS) * gq, qbase) * (MLA_SCALE * LOG2E)).astype(BF16)
        q_ref[:, sl] = qt
        if decode:
            qabs_ref[:, hd * MLA_KV_LORA:(hd + 1) * MLA_KV_LORA] = _dot(qt, wukt_ref[hd]).astype(BF16)

    ckv_b = ckv.astype(BF16)
    kn = _dot(ckv_b, wuk_ref[...])
    krsq = sum_sq(kr)
    krot = rope(kr * gkr_ref[...], cos)
    for hd in range(MLA_HEADS):
        sl = slice(hd * HEAD_PAD, (hd + 1) * HEAD_PAD)
        blk = kn[:, sl]
        ms = (sum_sq(blk) + krsq) * (1.0 / MLA_QK)
        k_ref[:, sl] = ((blk + krot) * lax.rsqrt(ms + EPS)).astype(BF16)
    lane = lax.broadcasted_iota(jnp.int32, (1, HEAD_PAD), 1)
    vv = _dot(ckv_b, wuv_ref[...])
    for hd in range(MLA_HEADS):
        vt_ref[hd] = jnp.where(lane < MLA_V, vv[:, hd * HEAD_PAD:(hd + 1) * HEAD_PAD], 1.0).T.astype(BF16)


def _prep_a(x, gmix, wa, tabs, tm, decode):
    m = x.shape[0]
    n_tab = tabs[0].shape[0] // tm
    row = lambda i: (i, 0)
    tab = lambda i: (i % n_tab, 0)
    hp = MLA_HEADS * HEAD_PAD
    in_specs = [
        pl.BlockSpec((tm, D_MODEL), row), _const_spec((1, D_MODEL)), _const_spec((D_MODEL, A_IN_PAD)),
        _const_spec((1, MLA_Q_LORA)), _const_spec((1, MLA_KV_LORA)), _const_spec((MLA_Q_LORA, hp)),
        _const_spec((1, HEAD_PAD)), _const_spec((1, HEAD_PAD)), _const_spec((1, HEAD_PAD)),
        _const_spec((MLA_KV_LORA, hp)), _const_spec((MLA_KV_LORA, hp)),
        pl.BlockSpec((tm, HEAD_PAD), tab), pl.BlockSpec((tm, HEAD_PAD), tab), pl.BlockSpec((tm, HEAD_PAD), tab),
    ]
    args = [x, gmix, wa["w_in"], wa["g_q_lora"], wa["g_kv_lora"], wa["w_uq"], wa["g_q"], wa["g_kn"], wa["g_kr"],
            wa["w_uk"], wa["w_uv"], *tabs]
    out_shape = [
        jax.ShapeDtypeStruct((m, POOL_WIDTH), F32), jax.ShapeDtypeStruct((m, MLA_KV_LORA), F32),
        jax.ShapeDtypeStruct((m, MLA_ROPE), F32), jax.ShapeDtypeStruct((m, hp), BF16),
        jax.ShapeDtypeStruct((m, hp), BF16), jax.ShapeDtypeStruct((MLA_HEADS, HEAD_PAD, m), BF16),
    ]
    out_specs = [
        pl.BlockSpec((tm, POOL_WIDTH), row), pl.BlockSpec((tm, MLA_KV_LORA), row),
        pl.BlockSpec((tm, MLA_ROPE), row), pl.BlockSpec((tm, hp), row),
        pl.BlockSpec((tm, hp), row), pl.BlockSpec((MLA_HEADS, HEAD_PAD, tm), lambda i: (0, 0, i)),
    ]
    if decode:
        in_specs.append(_const_spec((MLA_HEADS, HEAD_PAD, MLA_KV_LORA)))
        args.append(wa["w_uk_t"])
        out_shape.append(jax.ShapeDtypeStruct((m, MLA_HEADS * MLA_KV_LORA), BF16))
        out_specs.append(pl.BlockSpec((tm, MLA_HEADS * MLA_KV_LORA), row))
    return pl.pallas_call(
        functools.partial(_prep_a_kernel, decode=decode),
        grid=(m // tm,), in_specs=in_specs, out_specs=out_specs, out_shape=out_shape,
        compiler_params=_cparams("parallel"), name="prep_a_dec" if decode else "prep_a",
    )(*args)


def _pool_mix(diffs, wp_ref, sp_ref, o_ref):
    for g in range(POOL_GROUPS):
        sl = slice(g * POOL_GROUP_DIM, (g + 1) * POOL_GROUP_DIM)
        o_ref[:, sl] = _dot(diffs[g].astype(BF16), wp_ref[g]) * sp_ref[:, sl]


def _pool_prompt_kernel(u_ref, halo_ref, wp_ref, sp_ref, o_ref, ext_ref):
    j = pl.program_id(1)
    tm = u_ref.shape[0]
    ext_ref[:POOL_HALO, :] = jnp.where(j > 0, halo_ref[...], 0.0)
    ext_ref[POOL_HALO:, :] = u_ref[...]
    pos = j * tm + lax.broadcasted_iota(jnp.int32, (tm, POOL_GROUP_DIM), 0)
    diffs = []
    for g, w in enumerate(POOL_WINDOWS):
        sl = slice(g * POOL_GROUP_DIM, (g + 1) * POOL_GROUP_DIM)
        cur = ext_ref[POOL_HALO:, sl]
        acc = cur
        for k in range(1, w):
            acc = acc + ext_ref[POOL_HALO - k:POOL_HALO - k + tm, sl]
        cnt = jnp.minimum(pos + 1, w).astype(F32)
        diffs.append(acc / cnt - cur)
    _pool_mix(diffs, wp_ref, sp_ref, o_ref)


def _pool_prompt(u, wp, sp, seq, tm):
    m = u.shape[0]
    nj = seq // tm
    hb = tm // POOL_HALO
    return pl.pallas_call(
        _pool_prompt_kernel,
        grid=(m // seq, nj),
        in_specs=[
            pl.BlockSpec((tm, POOL_WIDTH), lambda b, j: (b * nj + j, 0)),
            pl.BlockSpec((POOL_HALO, POOL_WIDTH), lambda b, j: (jnp.maximum((b * nj + j) * hb - 1, 0), 0)),
            _const_spec((POOL_GROUPS, POOL_GROUP_DIM, POOL_GROUP_DIM)), _const_spec((1, POOL_WIDTH)),
        ],
        out_specs=pl.BlockSpec((tm, POOL_WIDTH), lambda b, j: (b * nj + j, 0)),
        out_shape=jax.ShapeDtypeStruct((m, POOL_WIDTH), F32),
        scratch_shapes=[pltpu.VMEM((tm + POOL_HALO, POOL_WIDTH), F32)],
        compiler_params=_cparams("parallel", "parallel"), name="pool_prompt",
    )(u, u, wp, sp)


def _pool_sample_kernel(st_ref, u_ref, wp_ref, sp_ref, o_ref):
    diffs = []
    for g, w in enumerate(POOL_WINDOWS):
        sl = slice(g * POOL_GROUP_DIM, (g + 1) * POOL_GROUP_DIM)
        cur = u_ref[:, sl]
        acc = cur
        for k in range(1, w):
            acc = acc + st_ref[POOL_STATE - k, :, sl]
        diffs.append(acc / float(w) - cur)
    _pool_mix(diffs, wp_ref, sp_ref, o_ref)


def _pool_sample(state_t, u, wp, sp):
    m = u.shape[0]
    return pl.pallas_call(
        _pool_sample_kernel,
        grid=(1,),
        in_specs=[_const_spec(state_t.shape), _const_spec(u.shape),
                  _const_spec((POOL_GROUPS, POOL_GROUP_DIM, POOL_GROUP_DIM)), _const_spec((1, POOL_WIDTH))],
        out_specs=_const_spec((m, POOL_WIDTH)),
        out_shape=jax.ShapeDtypeStruct((m, POOL_WIDTH), F32),
        compiler_params=_cparams("arbitrary"), name="pool_sample",
    )(state_t, u, wp, sp)


def _flash_t_step(k_blk, q, vt_blk, m_prev, acc_prev, add=None, mask=None):
    s = _dot_nt(k_blk, q)
    if add is not None:
        step = k_blk.shape[0] // len(add)
        s = jnp.concatenate([s[j * step:(j + 1) * step] + row for j, row in enumerate(add)], axis=0)
    if mask is not None:
        s = jnp.where(mask, s, NEG)
    m_new = jnp.maximum(m_prev, jnp.max(s, axis=0, keepdims=True))
    p = jnp.exp2(s - m_new)
    acc = jnp.exp2(m_prev - m_new) * acc_prev + _dot(vt_blk, p.astype(BF16))
    return m_new, acc


def _mla_prompt_kernel(q_ref, k_ref, vt_ref, o_ref, m_scr, acc_scr, *, tk):
    qi = pl.program_id(1)
    tq = q_ref.shape[0]
    per_q = tq // tk
    key_in_tile = lax.broadcasted_iota(jnp.int32, (tk, tq), 0)
    query_in_tile = lax.broadcasted_iota(jnp.int32, (tk, tq), 1)
    m_scr[...] = jnp.full(m_scr.shape, NEG, F32)
    acc_scr[...] = jnp.zeros(acc_scr.shape, F32)

    def block(kb, mask):
        start = pl.multiple_of(kb * tk, tk)
        for hd in range(MLA_HEADS):
            sl = slice(hd * HEAD_PAD, (hd + 1) * HEAD_PAD)
            m_new, acc = _flash_t_step(k_ref[pl.ds(start, tk), sl], q_ref[:, sl], vt_ref[hd, :, pl.ds(start, tk)],
                                       m_scr[hd:hd + 1, :], acc_scr[hd], mask=mask)
            m_scr[hd:hd + 1, :] = m_new
            acc_scr[hd] = acc

    def past(kb, carry):
        block(kb, None)
        return carry

    lax.fori_loop(0, qi * per_q, past, 0)
    for j in range(per_q):
        block(qi * per_q + j, key_in_tile + j * tk <= query_in_tile)
    for hd in range(MLA_HEADS):
        acc = acc_scr[hd]
        o = (acc / acc[MLA_V:MLA_V + 1, :]).T
        o_ref[:, hd * MLA_V:(hd + 1) * MLA_V] = o[:, :MLA_V]


def _mla_prompt(q, k, vt, seq, tq, tk):
    m = q.shape[0]
    nq = seq // tq
    hp = MLA_HEADS * HEAD_PAD
    return pl.pallas_call(
        functools.partial(_mla_prompt_kernel, tk=tk),
        grid=(m // seq, nq),
        in_specs=[
            pl.BlockSpec((tq, hp), lambda b, i: (b * nq + i, 0)),
            pl.BlockSpec((seq, hp), lambda b, i: (b, 0)),
            pl.BlockSpec((MLA_HEADS, HEAD_PAD, seq), lambda b, i: (0, 0, b)),
        ],
        out_specs=pl.BlockSpec((tq, MLA_HEADS * MLA_V), lambda b, i: (b * nq + i, 0)),
        out_shape=jax.ShapeDtypeStruct((m, MLA_HEADS * MLA_V), F32),
        scratch_shapes=[pltpu.VMEM((MLA_HEADS, tq), F32), pltpu.VMEM((MLA_HEADS, HEAD_PAD, tq), F32)],
        compiler_params=_cparams("parallel", "arbitrary"), name="mla_prompt",
    )(q, k, vt)


def _page_copies(pt_ref, b, n_pages, srcs, bufs, sems, token_axes, slot):
    out = []
    for j in range(n_pages):
        pg = pt_ref[b, j]
        for src, buf, sem, axis in zip(srcs, bufs, sems, token_axes):
            window = pl.ds(j * PAGE, PAGE)
            dst = buf.at[slot, window] if axis == 0 else buf.at[slot, :, window]
            out.append(pltpu.make_async_copy(src.at[pg], dst, sem.at[slot]))
    return out


def _paged_prefetch(pt_ref, n_pages, srcs, bufs, sems, token_axes):
    b = pl.program_id(0)
    nb = pl.num_programs(0)
    slot = b % 2
    copies = functools.partial(_page_copies, pt_ref, n_pages=n_pages, srcs=srcs, bufs=bufs, sems=sems,
                               token_axes=token_axes)

    @pl.when(b == 0)
    def _():
        for c in copies(b=0, slot=0):
            c.start()

    @pl.when(b + 1 < nb)
    def _():
        for c in copies(b=b + 1, slot=1 - slot):
            c.start()

    for c in copies(b=b, slot=slot):
        c.wait()
    return slot


def _mla_decode_kernel(pt_ref, q_ref, qabs_ref, knew_ref, cnew_ref, gkr_ref, wukt_ref, cos_ref, sin_ref,
                       ckv_hbm, kr_hbm, o_ref, ckv_buf, kr_buf, s_scr, sem_c, sem_r, *, n_pages, chunk):
    t_past = n_pages * PAGE
    slot = _paged_prefetch(pt_ref, n_pages, (ckv_hbm, kr_hbm), (ckv_buf, kr_buf), (sem_c, sem_r), (0, 1))

    q = q_ref[0].astype(F32)
    s_new = jnp.sum(q * knew_ref[0].astype(F32), axis=-1, keepdims=True)
    lane = lax.broadcasted_iota(jnp.int32, (MLA_HEADS, HEAD_PAD), 1)
    first = (lane >= MLA_NOPE) & (lane < MLA_NOPE + 16)
    second = (lane >= MLA_NOPE + 16) & (lane < MLA_QK)
    psi2 = jnp.where(first, pltpu.roll(q, HEAD_PAD - 16, 1), 0.0) - jnp.where(second, pltpu.roll(q, 16, 1), 0.0)
    gkr = gkr_ref[...]
    psi1 = (q * gkr)[:, MLA_NOPE:MLA_QK].astype(BF16)
    psi2 = (psi2 * gkr)[:, MLA_NOPE:MLA_QK].astype(BF16)
    lhs = jnp.concatenate([wukt_ref[...], qabs_ref[0]], axis=0)
    n_up = MLA_HEADS * MLA_NOPE

    def score_chunk(c, _):
        start = pl.multiple_of(c * chunk, chunk)
        ckv_c = ckv_buf[slot, pl.ds(start, chunk), :].astype(BF16)
        r = _dot_nt(lhs, ckv_c)
        kn = r[:n_up]
        k2 = jnp.sum((kn * kn).reshape(MLA_HEADS, MLA_NOPE, chunk), axis=1)
        kr_c = kr_buf[slot, :, pl.ds(start, chunk)]
        cos_c = cos_ref[:, pl.ds(start, chunk)]
        sin_c = sin_ref[:, pl.ds(start, chunk)]
        s_rope = _dot(psi1, (kr_c * cos_c).astype(BF16)) + _dot(psi2, (kr_c * sin_c).astype(BF16))
        kr2 = jnp.sum(kr_c * kr_c, axis=0, keepdims=True)
        inv = lax.rsqrt((k2 + kr2) * (1.0 / MLA_QK) + EPS)
        s_scr[:, pl.ds(start, chunk)] = (r[n_up:n_up + MLA_HEADS] + s_rope) * inv
        return 0

    lax.fori_loop(0, t_past // chunk, score_chunk, 0)
    s = s_scr[...]
    m = jnp.maximum(jnp.max(s, axis=-1, keepdims=True), s_new)
    p = jnp.exp2(s - m)
    p_new = jnp.exp2(s_new - m)
    denom = jnp.sum(p, axis=-1, keepdims=True) + p_new
    s_scr[...] = p

    def pv_chunk(c, acc):
        start = pl.multiple_of(c * chunk, chunk)
        ckv_c = ckv_buf[slot, pl.ds(start, chunk), :].astype(BF16)
        return acc + _dot(s_scr[:, pl.ds(start, chunk)].astype(BF16), ckv_c)

    acc = lax.fori_loop(0, t_past // chunk, pv_chunk, jnp.zeros((MLA_HEADS, MLA_KV_LORA), F32))
    o_ref[0] = (acc + p_new * cnew_ref[0]) / denom


def _mla_decode(page_table, q, qabs16, knew, cnew, gkr, wukt2, cos32, sin32, cache_ckv, cache_kr, chunk):
    nb, n_pages = page_table.shape
    t_past = n_pages * PAGE
    kern = functools.partial(_mla_decode_kernel, n_pages=n_pages, chunk=chunk)
    b3 = lambda b, pt: (b, 0, 0)
    grid_spec = pltpu.PrefetchScalarGridSpec(
        num_scalar_prefetch=1, grid=(nb,),
        in_specs=[
            pl.BlockSpec((1, MLA_HEADS, HEAD_PAD), b3), pl.BlockSpec((1, 16, MLA_KV_LORA), b3),
            pl.BlockSpec((1, MLA_HEADS, HEAD_PAD), b3), pl.BlockSpec((1, 1, MLA_KV_LORA), b3),
            _const_spec((1, HEAD_PAD)), _const_spec((MLA_HEADS * MLA_NOPE, MLA_KV_LORA)),
            _const_spec((MLA_ROPE, t_past)), _const_spec((MLA_ROPE, t_past)),
            pl.BlockSpec(memory_space=pl.ANY), pl.BlockSpec(memory_space=pl.ANY),
        ],
        out_specs=pl.BlockSpec((1, MLA_HEADS, MLA_KV_LORA), b3),
        scratch_shapes=[
            pltpu.VMEM((2, t_past, MLA_KV_LORA), F32), pltpu.VMEM((2, MLA_ROPE, t_past), F32),
            pltpu.VMEM((MLA_HEADS, t_past), F32),
            pltpu.SemaphoreType.DMA((2,)), pltpu.SemaphoreType.DMA((2,)),
        ],
    )
    return pl.pallas_call(
        kern, grid_spec=grid_spec,
        out_shape=jax.ShapeDtypeStruct((nb, MLA_HEADS, MLA_KV_LORA), F32),
        compiler_params=_cparams("arbitrary"), name="mla_decode",
    )(page_table, q, qabs16, knew, cnew, gkr, wukt2, cos32, sin32, cache_ckv, cache_kr)


def _out_ffn_kernel(*refs, n_parts, latent):
    parts = refs[:n_parts]
    x_ref = refs[n_parts]
    idx = n_parts + 1
    if latent:
        wuv_ref = refs[idx]
        idx += 1
    wouts = refs[idx:idx + n_parts]
    gffn_ref, wgu_ref, wd_ref, o_ref, h_scr = refs[idx + n_parts:]
    mix = jnp.zeros(x_ref.shape, F32)
    for p_ref, w_ref in zip(parts, wouts):
        if latent and p_ref is parts[-1]:
            for hd in range(MLA_HEADS):
                lat = p_ref[:, hd * MLA_KV_LORA:(hd + 1) * MLA_KV_LORA].astype(BF16)
                a = _dot(lat, wuv_ref[hd]).astype(BF16)
                mix = mix + _dot(a, w_ref[hd * MLA_V:(hd + 1) * MLA_V, :])
        else:
            mix = mix + _dot(p_ref[...].astype(BF16), w_ref[...])
    x1 = x_ref[...] + mix
    h_scr[...] = _rms(x1, gffn_ref[...]).astype(BF16)
    acc = jnp.zeros_like(x1)
    for c in range(FFN_HIDDEN // FFN_CHUNK):
        lo = c * FFN_CHUNK
        gate = _dot(h_scr[...], wgu_ref[:, lo:lo + FFN_CHUNK])
        up = _dot(h_scr[...], wgu_ref[:, FFN_HIDDEN + lo:FFN_HIDDEN + lo + FFN_CHUNK])
        act = (gate / (1.0 + jnp.exp(-gate)) * up).astype(BF16)
        acc = acc + _dot(act, wd_ref[lo:lo + FFN_CHUNK, :])
    o_ref[...] = x1 + acc


def _out_ffn(parts, x, wouts, gffn, wgu, wd, tm, wuv=None):
    m = x.shape[0]
    row = lambda i: (i, 0)
    in_specs = [pl.BlockSpec((tm, p.shape[1]), row) for p in parts] + [pl.BlockSpec((tm, D_MODEL), row)]
    args = list(parts) + [x]
    if wuv is not None:
        in_specs.append(_const_spec(wuv.shape))
        args.append(wuv)
    in_specs += [_const_spec(w.shape) for w in wouts]
    in_specs += [_const_spec((1, D_MODEL)), _const_spec(wgu.shape), _const_spec(wd.shape)]
    args += list(wouts) + [gffn, wgu, wd]
    return pl.pallas_call(
        functools.partial(_out_ffn_kernel, n_parts=len(parts), latent=wuv is not None),
        grid=(m // tm,), in_specs=in_specs,
        out_specs=pl.BlockSpec((tm, D_MODEL), row),
        out_shape=jax.ShapeDtypeStruct((m, D_MODEL), F32),
        scratch_shapes=[pltpu.VMEM((tm, D_MODEL), BF16)],
        compiler_params=_cparams("parallel"), name="out_ffn",
    )(*args)


def _prep_c_kernel(x_ref, gmix_ref, wqkv_ref, gq_ref, gk_ref, seg_ref, ca_ref, cb1_ref, cb2_ref,
                   kf_ref, vf_ref, q_ref, k_ref, vt_ref):
    h = _rms(x_ref[...], gmix_ref[...]).astype(BF16)
    qkv = _dot(h, wqkv_ref[...])
    ca = ca_ref[...]
    cb1 = cb1_ref[...]
    cb2 = cb2_ref[...]
    seg = seg_ref[...]

    def norm_rope(t, g):
        ms = _dot((t * t).astype(BF16), seg) * (1.0 / C_HEAD_DIM)
        t = t * lax.rsqrt(ms + EPS) * g
        return t * ca + pltpu.roll(t, LANES - C_ROT // 2, 1) * cb1 + pltpu.roll(t, C_ROT // 2, 1) * cb2

    lane = lax.broadcasted_iota(jnp.int32, (1, LANES), 1)
    low = lane < C_HEAD_DIM
    for j in range(C_Q_WIDTH // LANES):
        qt = norm_rope(qkv[:, j * LANES:(j + 1) * LANES], gq_ref[...])
        q_ref[j] = (qt * (C_SCALE * LOG2E)).astype(BF16)
    for j in range(C_KV_WIDTH // LANES):
        kt = norm_rope(qkv[:, C_Q_WIDTH + j * LANES:C_Q_WIDTH + (j + 1) * LANES], gk_ref[...])
        vt = qkv[:, C_Q_WIDTH + C_KV_WIDTH + j * LANES:C_Q_WIDTH + C_KV_WIDTH + (j + 1) * LANES]
        kf_ref[:, j * LANES:(j + 1) * LANES] = kt
        vf_ref[:, j * LANES:(j + 1) * LANES] = vt
        k_ref[2 * j] = jnp.where(low, kt, 0.0).astype(BF16)
        k_ref[2 * j + 1] = jnp.where(low, 0.0, kt).astype(BF16)
        vt_ref[2 * j] = jnp.where(low, vt, 1.0).T.astype(BF16)
        vt_ref[2 * j + 1] = jnp.where(low, 1.0, vt).T.astype(BF16)


def _prep_c(x, gmix, wc, tabs, tm):
    m = x.shape[0]
    n_tab = tabs[0].shape[0] // tm
    row = lambda i: (i, 0)
    tab = lambda i: (i % n_tab, 0)
    nq = C_Q_WIDTH // LANES
    return pl.pallas_call(
        _prep_c_kernel,
        grid=(m // tm,),
        in_specs=[
            pl.BlockSpec((tm, D_MODEL), row), _const_spec((1, D_MODEL)),
            _const_spec((D_MODEL, C_Q_WIDTH + 2 * C_KV_WIDTH)),
            _const_spec((1, LANES)), _const_spec((1, LANES)), _const_spec((LANES, LANES)),
            pl.BlockSpec((tm, LANES), tab), pl.BlockSpec((tm, LANES), tab), pl.BlockSpec((tm, LANES), tab),
        ],
        out_specs=[
            pl.BlockSpec((tm, C_KV_WIDTH), row), pl.BlockSpec((tm, C_KV_WIDTH), row),
            pl.BlockSpec((nq, tm, LANES), lambda i: (0, i, 0)),
            pl.BlockSpec((C_KV_HEADS, tm, LANES), lambda i: (0, i, 0)),
            pl.BlockSpec((C_KV_HEADS, LANES, tm), lambda i: (0, 0, i)),
        ],
        out_shape=[
            jax.ShapeDtypeStruct((m, C_KV_WIDTH), F32), jax.ShapeDtypeStruct((m, C_KV_WIDTH), F32),
            jax.ShapeDtypeStruct((nq, m, LANES), BF16),
            jax.ShapeDtypeStruct((C_KV_HEADS, m, LANES), BF16),
            jax.ShapeDtypeStruct((C_KV_HEADS, LANES, m), BF16),
        ],
        compiler_params=_cparams("parallel"), name="prep_c",
    )(x, gmix, wc["w_qkv"], wc["g_q"], wc["g_k"], wc["seg"], *tabs)


def _topk_rank_mask(g, valid, n_rows):
    row = lax.broadcasted_iota(jnp.int32, g.shape, 0)
    g = jnp.where(valid, g, NEG)
    cnt = jnp.zeros(g.shape, F32)
    for mrow in range(n_rows):
        gm = g[mrow:mrow + 1, :]
        cnt = cnt + jnp.where(row > mrow, jnp.where(gm >= g, 1.0, 0.0), jnp.where(gm > g, 1.0, 0.0))
    return jnp.where(valid & (cnt < MOBA_TOPK), 1.0, 0.0)


def _moba_prompt_kernel(q_ref, k_ref, vt_ref, o_ref, m_scr, acc_scr, bias_scr, *, n_blk):
    qi = pl.program_id(1)
    blk = MOBA_BLOCK
    nq = C_GROUP * blk
    causal = (lax.broadcasted_iota(jnp.int32, (blk, nq), 0)
              <= (lax.broadcasted_iota(jnp.int32, (blk, nq), 1) & (blk - 1)))
    cand = lax.broadcasted_iota(jnp.int32, (8, nq), 0)
    cand_k = lax.broadcasted_iota(jnp.int32, (8, LANES), 0)

    def queries(kv):
        pair = kv // 2
        return q_ref[pair * C_GROUP:(pair + 1) * C_GROUP].reshape(nq, LANES)

    for kv in range(C_KV_HEADS):
        kmean = jnp.zeros((8, LANES), F32)
        for n in range(n_blk):
            ksum = jnp.sum(k_ref[kv, n * blk:(n + 1) * blk, :].astype(F32), axis=0, keepdims=True)
            kmean = jnp.where(cand_k == n, ksum * (1.0 / blk), kmean)
        k_hi = kmean.astype(BF16)
        k_lo = (kmean - k_hi.astype(F32)).astype(BF16)
        q = queries(kv)
        gate_t = _dot_nt(k_hi, q) + _dot_nt(k_lo, q)
        bias_scr[kv] = (_topk_rank_mask(gate_t, cand < qi, 8) - 1.0) * (-NEG)
    m_scr[...] = jnp.full(m_scr.shape, NEG, F32)
    acc_scr[...] = jnp.zeros(acc_scr.shape, F32)

    def block(keys, add_rows, mask):
        for kv in range(C_KV_HEADS):
            add = None if add_rows is None else [bias_scr[kv, n:n + 1, :] for n in add_rows]
            m_new, acc = _flash_t_step(k_ref[kv, keys, :], queries(kv), vt_ref[kv, :, keys],
                                       m_scr[kv:kv + 1, :], acc_scr[kv], add=add, mask=mask)
            m_scr[kv:kv + 1, :] = m_new
            acc_scr[kv] = acc

    for n in range(0, n_blk - 1, 2):
        if n + 2 <= n_blk - 1:
            @pl.when(n + 1 < qi)
            def _():
                block(slice(n * blk, (n + 2) * blk), (n, n + 1), None)

        @pl.when(n + 1 == qi)
        def _():
            block(slice(n * blk, (n + 1) * blk), (n,), None)

    block(pl.ds(pl.multiple_of(qi * blk, blk), blk), None, causal)

    low = lax.broadcasted_iota(jnp.int32, (LANES, nq), 0) < C_HEAD_DIM
    for pair in range(C_KV_HEADS // 2):
        even = acc_scr[2 * pair]
        odd = acc_scr[2 * pair + 1]
        o_t = jnp.where(low, even / even[C_HEAD_DIM:C_HEAD_DIM + 1, :], odd / odd[0:1, :])
        for g in range(C_GROUP):
            tile = pair * C_GROUP + g
            o_ref[:, tile * LANES:(tile + 1) * LANES] = o_t[:, g * blk:(g + 1) * blk].T.astype(BF16)


def _moba_prompt(q, k, vt, seq):
    nq_t, m, _ = q.shape
    n_blk = seq // MOBA_BLOCK
    assert n_blk <= 8
    nq = C_GROUP * MOBA_BLOCK
    return pl.pallas_call(
        functools.partial(_moba_prompt_kernel, n_blk=n_blk),
        grid=(m // seq, n_blk),
        in_specs=[
            pl.BlockSpec((nq_t, MOBA_BLOCK, LANES), lambda b, i: (0, b * n_blk + i, 0)),
            pl.BlockSpec((C_KV_HEADS, seq, LANES), lambda b, i: (0, b, 0)),
            pl.BlockSpec((C_KV_HEADS, LANES, seq), lambda b, i: (0, 0, b)),
        ],
        out_specs=pl.BlockSpec((MOBA_BLOCK, nq_t * LANES), lambda b, i: (b * n_blk + i, 0)),
        out_shape=jax.ShapeDtypeStruct((m, nq_t * LANES), BF16),
        scratch_shapes=[pltpu.VMEM((C_KV_HEADS, nq), F32), pltpu.VMEM((C_KV_HEADS, LANES, nq), F32),
                        pltpu.VMEM((C_KV_HEADS, 8, nq), F32)],
        compiler_params=_cparams("parallel", "arbitrary"), name="moba_prompt",
    )(q, k, vt)


def _moba_decode_kernel(pt_ref, qbd_ref, knew_ref, vnew_ref, k_hbm, v_hbm, o_ref,
                        k_buf, v_buf, s_scr, sem_k, sem_v, *, n_pages, chunk):
    t_past = n_pages * PAGE
    n_blk = t_past // MOBA_BLOCK
    slot = _paged_prefetch(pt_ref, n_pages, (k_hbm, v_hbm), (k_buf, v_buf), (sem_k, sem_v), (1, 1))
    qbd = qbd_ref[0]
    s_new = jnp.sum(qbd.astype(F32) * knew_ref[0], axis=-1, keepdims=True)

    def score_chunk(c, _):
        start = pl.multiple_of(c * chunk, chunk)
        s_scr[:, pl.ds(start, chunk)] = _dot(qbd, k_buf[slot, :, pl.ds(start, chunk)].astype(BF16))
        return 0

    lax.fori_loop(0, t_past // chunk, score_chunk, 0)

    gates, maxes = [], []
    for n in range(n_blk):
        s_n = s_scr[:, n * MOBA_BLOCK:(n + 1) * MOBA_BLOCK]
        gates.append(jnp.sum(s_n, axis=-1, keepdims=True))
        maxes.append(jnp.max(s_n, axis=-1, keepdims=True))
    sels = []
    for n in range(n_blk):
        cnt = jnp.zeros_like(gates[n])
        for mb in range(n_blk):
            if mb != n:
                beats = (gates[mb] >= gates[n]) if mb < n else (gates[mb] > gates[n])
                cnt = cnt + jnp.where(beats, 1.0, 0.0)
        sels.append(cnt < MOBA_TOPK)
    m = s_new
    for n in range(n_blk):
        m = jnp.maximum(m, jnp.where(sels[n], maxes[n], NEG))
    p_new = jnp.exp2(s_new - m)
    psum = jnp.zeros((C_HEADS, MOBA_BLOCK), F32)
    for n in range(n_blk):
        sl = slice(n * MOBA_BLOCK, (n + 1) * MOBA_BLOCK)
        p = jnp.where(sels[n], jnp.exp2(s_scr[:, sl] - m), 0.0)
        s_scr[:, sl] = p
        psum = psum + p
    denom = jnp.sum(psum, axis=-1, keepdims=True) + p_new

    def pv_chunk(c, acc):
        start = pl.multiple_of(c * chunk, chunk)
        return acc + _dot_nt(s_scr[:, pl.ds(start, chunk)].astype(BF16),
                             v_buf[slot, :, pl.ds(start, chunk)].astype(BF16))

    acc = lax.fori_loop(0, t_past // chunk, pv_chunk, jnp.zeros((C_HEADS, C_KV_WIDTH), F32))
    o_ref[0] = (acc + p_new * vnew_ref[0]) / denom


def _moba_decode(page_table, qbd, knew, vnew, cache_k, cache_v, chunk):
    nb, n_pages = page_table.shape
    t_past = n_pages * PAGE
    b3 = lambda b, pt: (b, 0, 0)
    grid_spec = pltpu.PrefetchScalarGridSpec(
        num_scalar_prefetch=1, grid=(nb,),
        in_specs=[
            pl.BlockSpec((1, C_HEADS, C_KV_WIDTH), b3), pl.BlockSpec((1, 1, C_KV_WIDTH), b3),
            pl.BlockSpec((1, 1, C_KV_WIDTH), b3),
            pl.BlockSpec(memory_space=pl.ANY), pl.BlockSpec(memory_space=pl.ANY),
        ],
        out_specs=pl.BlockSpec((1, C_HEADS, C_KV_WIDTH), b3),
        scratch_shapes=[
            pltpu.VMEM((2, C_KV_WIDTH, t_past), F32), pltpu.VMEM((2, C_KV_WIDTH, t_past), F32),
            pltpu.VMEM((C_HEADS, t_past), F32),
            pltpu.SemaphoreType.DMA((2,)), pltpu.SemaphoreType.DMA((2,)),
        ],
    )
    return pl.pallas_call(
        functools.partial(_moba_decode_kernel, n_pages=n_pages, chunk=chunk), grid_spec=grid_spec,
        out_shape=jax.ShapeDtypeStruct((nb, C_HEADS, C_KV_WIDTH), F32),
        compiler_params=_cparams("arbitrary"), name="moba_decode",
    )(page_table, qbd, knew, vnew, cache_k, cache_v)


def _rope_angles(pos, dim, theta):
    inv_freq = jnp.exp(jnp.arange(0, dim, 2, dtype=F32) * (-math.log(theta) / dim))
    ang = pos.astype(F32)[:, None] * inv_freq[None, :]
    return jnp.cos(ang), jnp.sin(ang)


def _mla_tables(pos):
    cos, sin = _rope_angles(pos, MLA_ROPE, MLA_THETA)
    z = lambda n: jnp.zeros((pos.shape[0], n), F32)
    half = MLA_ROPE // 2
    tail = HEAD_PAD - MLA_QK
    cos_t = jnp.concatenate([z(MLA_NOPE), cos, cos, z(tail)], axis=1)
    sina = jnp.concatenate([z(MLA_NOPE), -sin, z(half), z(tail)], axis=1)
    sinb = jnp.concatenate([z(MLA_NOPE), z(half), sin, z(tail)], axis=1)
    return cos_t, sina, sinb


def _moba_tables(pos):
    cos, sin = _rope_angles(pos, C_ROT, C_THETA)
    n = pos.shape[0]
    half = C_ROT // 2
    rest = C_HEAD_DIM - C_ROT
    one = jnp.ones((n, rest), F32)
    z = lambda k: jnp.zeros((n, k), F32)
    ca = jnp.concatenate([cos, cos, one] * 2, axis=1)
    cb1 = jnp.concatenate([-sin, z(half), z(rest)] * 2, axis=1)
    cb2 = jnp.concatenate([z(half), sin, z(rest)] * 2, axis=1)
    return ca, cb1, cb2


def _pad_lanes(a, start, total):
    pad = [(0, 0)] * (a.ndim - 1) + [(start, total - start - a.shape[-1])]
    return jnp.pad(a, pad)


def _layer_a_weights(w_in, g_q_lora, g_kv_lora, w_uq, w_uk, w_uv, g_q, g_k, w_pool, s_pool, w_out):
    o_kr = POOL_WIDTH + MLA_Q_LORA + MLA_KV_LORA
    w_in_p = jnp.concatenate([w_in[:, :o_kr], _pad_lanes(w_in[:, o_kr:], MLA_NOPE, HEAD_PAD)], axis=1)
    hp = MLA_HEADS * HEAD_PAD
    return {
        "w_in": w_in_p.astype(BF16),
        "g_q_lora": g_q_lora[None], "g_kv_lora": g_kv_lora[None],
        "w_uq": _pad_lanes(w_uq, 0, HEAD_PAD).reshape(MLA_Q_LORA, hp).astype(BF16),
        "w_uk": _pad_lanes(w_uk, 0, HEAD_PAD).reshape(MLA_KV_LORA, hp).astype(BF16),
        "w_uv": _pad_lanes(w_uv, 0, HEAD_PAD).reshape(MLA_KV_LORA, hp).astype(BF16),
        "w_uk_t": _pad_lanes(jnp.transpose(w_uk, (1, 0, 2)), 0, HEAD_PAD).transpose(0, 2, 1).astype(BF16),
        "w_uk_t2": jnp.transpose(w_uk, (1, 2, 0)).reshape(MLA_HEADS * MLA_NOPE, MLA_KV_LORA).astype(BF16),
        "w_uv_h": jnp.transpose(w_uv, (1, 0, 2)).astype(BF16),
        "g_q": _pad_lanes(g_q, 0, HEAD_PAD)[None],
        "g_kn": _pad_lanes(g_k[:MLA_NOPE], 0, HEAD_PAD)[None],
        "g_kr": _pad_lanes(g_k[MLA_NOPE:], MLA_NOPE, HEAD_PAD)[None],
        "w_pool": w_pool.astype(BF16), "s_pool": s_pool[None],
        "w_out_pool": w_out[:POOL_WIDTH].astype(BF16), "w_out_mla": w_out[POOL_WIDTH:].astype(BF16),
    }


def _moba_tile_heads():
    order = []
    for j in range(C_Q_WIDTH // LANES):
        p, g = divmod(j, C_GROUP)
        order += [(2 * p) * C_GROUP + g, (2 * p + 1) * C_GROUP + g]
    return order


def _layer_c_weights(w_qkv, g_q, g_k, w_o):
    order = jnp.array(_moba_tile_heads())
    wq = w_qkv[:, :C_Q_WIDTH].reshape(D_MODEL, C_HEADS, C_HEAD_DIM)[:, order].reshape(D_MODEL, C_Q_WIDTH)
    w_o_p = w_o.reshape(C_HEADS, C_HEAD_DIM, D_MODEL)[order].reshape(C_Q_WIDTH, D_MODEL)
    seg = jnp.kron(jnp.eye(LANES // C_HEAD_DIM, dtype=F32), jnp.ones((C_HEAD_DIM, C_HEAD_DIM), F32))
    return {
        "w_qkv": jnp.concatenate([wq, w_qkv[:, C_Q_WIDTH:]], axis=1).astype(BF16),
        "g_q": jnp.tile(g_q, 2)[None], "g_k": jnp.tile(g_k, 2)[None], "seg": seg.astype(BF16),
        "w_o": w_o_p.astype(BF16),
    }


def _pages_feature_major(cache):
    n = cache.shape[0]
    return jnp.transpose(cache, (0, 2, 3, 1)).reshape(n, C_KV_WIDTH, PAGE)


def _tiles_to_rows(t):
    return jnp.transpose(t, (1, 0, 2)).reshape(t.shape[1], t.shape[0] * LANES)


def kernel(x_prompt, x_sample, cache_mla_ckv, cache_mla_krope, state_pool, cache_moba_k, cache_moba_v,
           page_table, g_mix, g_ffn, w_in_a, g_q_lora, g_kv_lora, w_uq, w_uk, w_uv, g_mla_q, g_mla_k,
           w_pool, s_pool, w_out_a, w_qkv_c, g_moba_q, g_moba_k, w_o_c, w_gate_up, w_down):
    bp, seq, _ = x_prompt.shape
    bs = x_sample.shape[0]
    n_pages = page_table.shape[1]
    past = n_pages * PAGE
    depth = g_mix.shape[0]
    mp = bp * seq
    xp = x_prompt.reshape(mp, D_MODEL)
    xs = x_sample.reshape(bs, D_MODEL)
    tm_p = tm_f = next(t for t in TOKEN_TILES if seq % t == 0)
    tq = next(t for t in MLA_TILES if seq % t == 0)
    chunk_a = min(MLA_DECODE_CHUNK, past)
    chunk_c = min(MOBA_DECODE_CHUNK, past)

    pos_p = jnp.arange(seq, dtype=jnp.int32)
    pos_s = jnp.full((bs,), past, dtype=jnp.int32)
    wgu = w_gate_up.astype(BF16)
    wdn = w_down.astype(BF16)

    outs_p = {k: [] for k in ("ckv", "kr", "pool", "k", "v")}
    outs_s = {k: [] for k in ("ckv", "kr", "pool", "k", "v")}
    for layer in range(depth):
        i = layer // 2
        gmix = g_mix[layer][None]
        gffn = g_ffn[layer][None]
        if layer % 2 == 0:
            wa = _layer_a_weights(w_in_a[i], g_q_lora[i], g_kv_lora[i], w_uq[i], w_uk[i], w_uv[i],
                                  g_mla_q[i], g_mla_k[i], w_pool[i], s_pool[i], w_out_a[i])
            u, ckv, kr, q, k, v = _prep_a(xp, gmix, wa, _mla_tables(pos_p), tm_p, decode=False)
            pool_out = _pool_prompt(u, wa["w_pool"], wa["s_pool"], seq, tm_p)
            mla_out = _mla_prompt(q, k, v, seq, tq, tq)
            xp = _out_ffn([pool_out, mla_out], xp, [wa["w_out_pool"], wa["w_out_mla"]], gffn,
                          wgu[layer], wdn[layer], tm_f)
            outs_p["ckv"].append(ckv.reshape(bp, seq, MLA_KV_LORA))
            outs_p["kr"].append(kr.reshape(bp, seq, MLA_ROPE))
            outs_p["pool"].append(u.reshape(bp, seq, POOL_WIDTH)[:, seq - POOL_STATE:])
            u, ckv, kr, q, k, v, qabs = _prep_a(xs, gmix, wa, _mla_tables(pos_s), bs, decode=True)
            state = state_pool[i]
            pool_out = _pool_sample(jnp.transpose(state, (1, 0, 2)), u, wa["w_pool"], wa["s_pool"])
            cos_k, sin_k = _rope_angles(jnp.arange(past, dtype=jnp.int32), MLA_ROPE, MLA_THETA)
            qabs16 = jnp.pad(qabs.reshape(bs, MLA_HEADS, MLA_KV_LORA), ((0, 0), (0, 16 - MLA_HEADS), (0, 0)))
            o_lat = _mla_decode(
                page_table, q.reshape(bs, MLA_HEADS, HEAD_PAD), qabs16, k.reshape(bs, MLA_HEADS, HEAD_PAD),
                ckv.reshape(bs, 1, MLA_KV_LORA), wa["g_kr"], wa["w_uk_t2"],
                jnp.concatenate([cos_k, cos_k], axis=1).T, jnp.concatenate([sin_k, sin_k], axis=1).T,
                cache_mla_ckv[i], jnp.transpose(cache_mla_krope[i], (0, 2, 1)), chunk_a)
            xs = _out_ffn([pool_out, o_lat.reshape(bs, MLA_HEADS * MLA_KV_LORA)], xs,
                          [wa["w_out_pool"], wa["w_out_mla"]], gffn, wgu[layer], wdn[layer], bs,
                          wuv=wa["w_uv_h"])
            outs_s["ckv"].append(ckv.reshape(bs, 1, MLA_KV_LORA))
            outs_s["kr"].append(kr.reshape(bs, 1, MLA_ROPE))
            outs_s["pool"].append(jnp.concatenate([state[:, 1:], u[:, None]], axis=1))
        else:
            wc = _layer_c_weights(w_qkv_c[i], g_moba_q[i], g_moba_k[i], w_o_c[i])
            kf, vf, q, k, v = _prep_c(xp, gmix, wc, _moba_tables(pos_p), tm_p)
            o = _moba_prompt(q, k, v, seq)
            xp = _out_ffn([o], xp, [wc["w_o"]], gffn, wgu[layer], wdn[layer], tm_f)
            outs_p["k"].append(kf.reshape(bp, seq, C_KV_HEADS, C_HEAD_DIM))
            outs_p["v"].append(vf.reshape(bp, seq, C_KV_HEADS, C_HEAD_DIM))
            kf, vf, q, k, v = _prep_c(xs, gmix, wc, _moba_tables(pos_s), bs)
            qh = _tiles_to_rows(q).reshape(bs, C_Q_WIDTH // LANES, 2, C_HEAD_DIM)
            kv_of = jnp.array(_moba_tile_heads()).reshape(-1, 2) // C_GROUP
            onehot = (kv_of[:, :, None] == jnp.arange(C_KV_HEADS)[None, None, :]).astype(BF16)
            qbd = (qh[:, :, :, None, :] * onehot[None, :, :, :, None]).reshape(bs, C_HEADS, C_KV_WIDTH)
            o = _moba_decode(page_table, qbd, kf.reshape(bs, 1, C_KV_WIDTH), vf.reshape(bs, 1, C_KV_WIDTH),
                             _pages_feature_major(cache_moba_k[i]), _pages_feature_major(cache_moba_v[i]), chunk_c)
            o = (o.reshape(bs, C_Q_WIDTH // LANES, 2, C_KV_HEADS, C_HEAD_DIM)
                 * onehot[None, :, :, :, None].astype(F32)).sum(axis=3).reshape(bs, C_Q_WIDTH)
            xs = _out_ffn([o], xs, [wc["w_o"]], gffn, wgu[layer], wdn[layer], bs)
            outs_s["k"].append(kf.reshape(bs, 1, C_KV_HEADS, C_HEAD_DIM))
            outs_s["v"].append(vf.reshape(bs, 1, C_KV_HEADS, C_HEAD_DIM))

    st = lambda xs_: jnp.stack(xs_)
    return (xp.reshape(bp, seq, D_MODEL), xs.reshape(bs, 1, D_MODEL),
            st(outs_p["ckv"]), st(outs_p["kr"]), st(outs_p["pool"]), st(outs_p["k"]), st(outs_p["v"]),
            st(outs_s["ckv"]), st(outs_s["kr"]), st(outs_s["pool"]), st(outs_s["k"]), st(outs_s["v"]))
```

```python
import functools
import math

import jax
import jax.numpy as jnp
from jax import lax
from jax.experimental import pallas as pl
from jax.experimental.pallas import tpu as pltpu

F32 = jnp.float32
BF16 = jnp.bfloat16
LOG2E = math.log2(math.e)

D_MODEL = 1024
EPS = 1e-6
NEG = -1e30

POOL_WIDTH = 512
POOL_GROUPS = 4
POOL_GROUP_DIM = 128
POOL_WINDOWS = (2, 4, 8, 16)
POOL_STATE = 15
POOL_HALO = 16

MLA_HEADS = 8
MLA_NOPE = 64
MLA_ROPE = 32
MLA_QK = 96
MLA_V = 64
MLA_Q_LORA = 384
MLA_KV_LORA = 256
MLA_THETA = 10000.0
MLA_SCALE = MLA_QK ** -0.5
HEAD_PAD = 128
MLA_TILES = (1024, 512, 256)
A_IN_PAD = POOL_WIDTH + MLA_Q_LORA + MLA_KV_LORA + HEAD_PAD

C_HEADS = 16
C_KV_HEADS = 4
C_GROUP = 4
C_HEAD_DIM = 64
C_ROT = 16
C_THETA = 500000.0
C_SCALE = C_HEAD_DIM ** -0.5
MOBA_BLOCK = 256
MOBA_TOPK = 3
C_Q_WIDTH = C_HEADS * C_HEAD_DIM
C_KV_WIDTH = C_KV_HEADS * C_HEAD_DIM

FFN_HIDDEN = 2816
FFN_CHUNK = 256

PAGE = 128
LANES = 128
TOKEN_TILES = (1024, 512, 256)
MLA_DECODE_CHUNK = 4096
MOBA_DECODE_CHUNK = 8192
VMEM_LIMIT = 56 * 1024 * 1024


def _cparams(*sem):
    return pltpu.CompilerParams(dimension_semantics=sem, vmem_limit_bytes=VMEM_LIMIT)


def _const_spec(shape):
    nd = len(shape)
    return pl.BlockSpec(shape, lambda *_: (0,) * nd, pipeline_mode=pl.Buffered(1))


def _rms(x, g):
    ms = jnp.mean(x * x, axis=-1, keepdims=True)
    return x * lax.rsqrt(ms + EPS) * g


def _dot(a, b):
    return jnp.dot(a, b, preferred_element_type=F32)


def _dot_nt(a, b):
    return lax.dot_general(a, b, (((1,), (1,)), ((), ())), preferred_element_type=F32)


def _prep_a_kernel(x_ref, gmix_ref, win_ref, gql_ref, gkvl_ref, wuq_ref, gq_ref, gkn_ref, gkr_ref,
                   wuk_ref, wuv_ref, cos_ref, sina_ref, sinb_ref, *rest, decode):
    if decode:
        wukt_ref, u_ref, ckv_ref, kr_ref, q_ref, k_ref, vt_ref, qabs_ref = rest
    else:
        u_ref, ckv_ref, kr_ref, q_ref, k_ref, vt_ref = rest
    h = _rms(x_ref[...], gmix_ref[...]).astype(BF16)
    hw = _dot(h, win_ref[...])
    u_ref[...] = hw[:, :POOL_WIDTH]
    o_q = POOL_WIDTH
    o_kv = o_q + MLA_Q_LORA
    o_kr = o_kv + MLA_KV_LORA
    cq = _rms(hw[:, o_q:o_kv], gql_ref[...]).astype(BF16)
    ckv = _rms(hw[:, o_kv:o_kr], gkvl_ref[...])
    ckv_ref[...] = ckv
    kr = hw[:, o_kr:o_kr + HEAD_PAD]
    kr_ref[...] = kr[:, MLA_NOPE:MLA_QK]
    cos = cos_ref[...]
    sina = sina_ref[...]
    sinb = sinb_ref[...]

    def rope(t, base):
        return t * base + pltpu.roll(t, HEAD_PAD - 16, 1) * sina + pltpu.roll(t, 16, 1) * sinb

    ones = jnp.ones((HEAD_PAD, HEAD_PAD), BF16)

    def sum_sq(t):
        return _dot((t * t).astype(BF16), ones)

    q = _dot(cq, wuq_ref[...])
    gq = gq_ref[...]
    qbase = gkn_ref[...] + cos
    for hd in range(MLA_HEADS):
        sl = slice(hd * HEAD_PAD, (hd + 1) * HEAD_PAD)
        blk = q[:, sl]
        ms = sum_sq(blk) * (1.0 / MLA_QK)
        qt = (rope(blk * lax.rsqrt(ms + EPS) * gq, qbase) * (MLA_SCALE * LOG2E)).astype(BF16)
        q_ref[:, sl] = qt
        if decode:
            qabs_ref[:, hd * MLA_KV_LORA:(hd + 1) * MLA_KV_LORA] = _dot(qt, wukt_ref[hd]).astype(BF16)

    ckv_b = ckv.astype(BF16)
    kn = _dot(ckv_b, wuk_ref[...])
    krsq = sum_sq(kr)
    krot = rope(kr * gkr_ref[...], cos)
    for hd in range(MLA_HEADS):
        sl = slice(hd * HEAD_PAD, (hd + 1) * HEAD_PAD)
        blk = kn[:, sl]
        ms = (sum_sq(blk) + krsq) * (1.0 / MLA_QK)
        k_ref[:, sl] = ((blk + krot) * lax.rsqrt(ms + EPS)).astype(BF16)
    lane = lax.broadcasted_iota(jnp.int32, (1, HEAD_PAD), 1)
    vv = _dot(ckv_b, wuv_ref[...])
    for hd in range(MLA_HEADS):
        vt_ref[hd] = jnp.where(lane < MLA_V, vv[:, hd * HEAD_PAD:(hd + 1) * HEAD_PAD], 1.0).T.astype(BF16)


def _prep_a(x, gmix, wa, tabs, tm, decode):
    m = x.shape[0]
    n_tab = tabs[0].shape[0] // tm
    row = lambda i: (i, 0)
    tab = lambda i: (i % n_tab, 0)
    hp = MLA_HEADS * HEAD_PAD
    in_specs = [
        pl.BlockSpec((tm, D_MODEL), row), _const_spec((1, D_MODEL)), _const_spec((D_MODEL, A_IN_PAD)),
        _const_spec((1, MLA_Q_LORA)), _const_spec((1, MLA_KV_LORA)), _const_spec((MLA_Q_LORA, hp)),
        _const_spec((1, HEAD_PAD)), _const_spec((1, HEAD_PAD)), _const_spec((1, HEAD_PAD)),
        _const_spec((MLA_KV_LORA, hp)), _const_spec((MLA_KV_LORA, hp)),
        pl.BlockSpec((tm, HEAD_PAD), tab), pl.BlockSpec((tm, HEAD_PAD), tab), pl.BlockSpec((tm, HEAD_PAD), tab),
    ]
    args = [x, gmix, wa["w_in"], wa["g_q_lora"], wa["g_kv_lora"], wa["w_uq"], wa["g_q"], wa["g_kn"], wa["g_kr"],
            wa["w_uk"], wa["w_uv"], *tabs]
    out_shape = [
        jax.ShapeDtypeStruct((m, POOL_WIDTH), F32), jax.ShapeDtypeStruct((m, MLA_KV_LORA), F32),
        jax.ShapeDtypeStruct((m, MLA_ROPE), F32), jax.ShapeDtypeStruct((m, hp), BF16),
        jax.ShapeDtypeStruct((m, hp), BF16), jax.ShapeDtypeStruct((MLA_HEADS, HEAD_PAD, m), BF16),
    ]
    out_specs = [
        pl.BlockSpec((tm, POOL_WIDTH), row), pl.BlockSpec((tm, MLA_KV_LORA), row),
        pl.BlockSpec((tm, MLA_ROPE), row), pl.BlockSpec((tm, hp), row),
        pl.BlockSpec((tm, hp), row), pl.BlockSpec((MLA_HEADS, HEAD_PAD, tm), lambda i: (0, 0, i)),
    ]
    if decode:
        in_specs.append(_const_spec((MLA_HEADS, HEAD_PAD, MLA_KV_LORA)))
        args.append(wa["w_uk_t"])
        out_shape.append(jax.ShapeDtypeStruct((m, MLA_HEADS * MLA_KV_LORA), BF16))
        out_specs.append(pl.BlockSpec((tm, MLA_HEADS * MLA_KV_LORA), row))
    return pl.pallas_call(
        functools.partial(_prep_a_kernel, decode=decode),
        grid=(m // tm,), in_specs=in_specs, out_specs=out_specs, out_shape=out_shape,
        compiler_params=_cparams("parallel"), name="prep_a_dec" if decode else "prep_a",
    )(*args)


def _pool_mix(diffs, wp_ref, sp_ref, o_ref):
    for g in range(POOL_GROUPS):
        sl = slice(g * POOL_GROUP_DIM, (g + 1) * POOL_GROUP_DIM)
        o_ref[:, sl] = _dot(diffs[g].astype(BF16), wp_ref[g]) * sp_ref[:, sl]


def _pool_prompt_kernel(u_ref, halo_ref, wp_ref, sp_ref, o_ref, ext_ref):
    j = pl.program_id(1)
    tm = u_ref.shape[0]
    ext_ref[:POOL_HALO, :] = jnp.where(j > 0, halo_ref[...], 0.0)
    ext_ref[POOL_HALO:, :] = u_ref[...]
    pos = j * tm + lax.broadcasted_iota(jnp.int32, (tm, POOL_GROUP_DIM), 0)
    diffs = []
    for g, w in enumerate(POOL_WINDOWS):
        sl = slice(g * POOL_GROUP_DIM, (g + 1) * POOL_GROUP_DIM)
        cur = ext_ref[POOL_HALO:, sl]
        acc = cur
        for k in range(1, w):
            acc = acc + ext_ref[POOL_HALO - k:POOL_HALO - k + tm, sl]
        cnt = jnp.minimum(pos + 1, w).astype(F32)
        diffs.append(acc / cnt - cur)
    _pool_mix(diffs, wp_ref, sp_ref, o_ref)


def _pool_prompt(u, wp, sp, seq, tm):
    m = u.shape[0]
    nj = seq // tm
    hb = tm // POOL_HALO
    return pl.pallas_call(
        _pool_prompt_kernel,
        grid=(m // seq, nj),
        in_specs=[
            pl.BlockSpec((tm, POOL_WIDTH), lambda b, j: (b * nj + j, 0)),
            pl.BlockSpec((POOL_HALO, POOL_WIDTH), lambda b, j: (jnp.maximum((b * nj + j) * hb - 1, 0), 0)),
            _const_spec((POOL_GROUPS, POOL_GROUP_DIM, POOL_GROUP_DIM)), _const_spec((1, POOL_WIDTH)),
        ],
        out_specs=pl.BlockSpec((tm, POOL_WIDTH), lambda b, j: (b * nj + j, 0)),
        out_shape=jax.ShapeDtypeStruct((m, POOL_WIDTH), F32),
        scratch_shapes=[pltpu.VMEM((tm + POOL_HALO, POOL_WIDTH), F32)],
        compiler_params=_cparams("parallel", "parallel"), name="pool_prompt",
    )(u, u, wp, sp)


def _pool_sample_kernel(st_ref, u_ref, wp_ref, sp_ref, o_ref):
    diffs = []
    for g, w in enumerate(POOL_WINDOWS):
        sl = slice(g * POOL_GROUP_DIM, (g + 1) * POOL_GROUP_DIM)
        cur = u_ref[:, sl]
        acc = cur
        for k in range(1, w):
            acc = acc + st_ref[POOL_STATE - k, :, sl]
        diffs.append(acc / float(w) - cur)
    _pool_mix(diffs, wp_ref, sp_ref, o_ref)


def _pool_sample(state_t, u, wp, sp):
    m = u.shape[0]
    return pl.pallas_call(
        _pool_sample_kernel,
        grid=(1,),
        in_specs=[_const_spec(state_t.shape), _const_spec(u.shape),
                  _const_spec((POOL_GROUPS, POOL_GROUP_DIM, POOL_GROUP_DIM)), _const_spec((1, POOL_WIDTH))],
        out_specs=_const_spec((m, POOL_WIDTH)),
        out_shape=jax.ShapeDtypeStruct((m, POOL_WIDTH), F32),
        compiler_params=_cparams("arbitrary"), name="pool_sample",
    )(state_t, u, wp, sp)


def _flash_t_step(k_blk, q, vt_blk, m_prev, acc_prev, add=None, mask=None):
    s = _dot_nt(k_blk, q)
    if add is not None:
        step = k_blk.shape[0] // len(add)
        s = jnp.concatenate([s[j * step:(j + 1) * step] + row for j, row in enumerate(add)], axis=0)
    if mask is not None:
        s = jnp.where(mask, s, NEG)
    m_new = jnp.maximum(m_prev, jnp.max(s, axis=0, keepdims=True))
    p = jnp.exp2(s - m_new)
    acc = jnp.exp2(m_prev - m_new) * acc_prev + _dot(vt_blk, p.astype(BF16))
    return m_new, acc


def _mla_prompt_kernel(q_ref, k_ref, vt_ref, o_ref, m_scr, acc_scr, *, tk):
    qi = pl.program_id(1)
    tq = q_ref.shape[0]
    per_q = tq // tk
    key_in_tile = lax.broadcasted_iota(jnp.int32, (tk, tq), 0)
    query_in_tile = lax.broadcasted_iota(jnp.int32, (tk, tq), 1)
    m_scr[...] = jnp.full(m_scr.shape, NEG, F32)
    acc_scr[...] = jnp.zeros(acc_scr.shape, F32)

    def block(kb, mask):
        start = pl.multiple_of(kb * tk, tk)
        for hd in range(MLA_HEADS):
            sl = slice(hd * HEAD_PAD, (hd + 1) * HEAD_PAD)
            m_new, acc = _flash_t_step(k_ref[pl.ds(start, tk), sl], q_ref[:, sl], vt_ref[hd, :, pl.ds(start, tk)],
                                       m_scr[hd:hd + 1, :], acc_scr[hd], mask=mask)
            m_scr[hd:hd + 1, :] = m_new
            acc_scr[hd] = acc

    def past(kb, carry):
        block(kb, None)
        return carry

    lax.fori_loop(0, qi * per_q, past, 0)
    for j in range(per_q):
        block(qi * per_q + j, key_in_tile + j * tk <= query_in_tile)
    for hd in range(MLA_HEADS):
        acc = acc_scr[hd]
        o = (acc / acc[MLA_V:MLA_V + 1, :]).T
        o_ref[:, hd * MLA_V:(hd + 1) * MLA_V] = o[:, :MLA_V]


def _mla_prompt(q, k, vt, seq, tq, tk):
    m = q.shape[0]
    nq = seq // tq
    hp = MLA_HEADS * HEAD_PAD
    return pl.pallas_call(
        functools.partial(_mla_prompt_kernel, tk=tk),
        grid=(m // seq, nq),
        in_specs=[
            pl.BlockSpec((tq, hp), lambda b, i: (b * nq + i, 0)),
            pl.BlockSpec((seq, hp), lambda b, i: (b, 0)),
            pl.BlockSpec((MLA_HEADS, HEAD_PAD, seq), lambda b, i: (0, 0, b)),
        ],
        out_specs=pl.BlockSpec((tq, MLA_HEADS * MLA_V), lambda b, i: (b * nq + i, 0)),
        out_shape=jax.ShapeDtypeStruct((m, MLA_HEADS * MLA_V), F32),
        scratch_shapes=[pltpu.VMEM((MLA_HEADS, tq), F32), pltpu.VMEM((MLA_HEADS, HEAD_PAD, tq), F32)],
        compiler_params=_cparams("parallel", "arbitrary"), name="mla_prompt",
    )(q, k, vt)


def _page_copies(pt_ref, b, n_pages, srcs, bufs, sems, token_axes, slot):
    out = []
    for j in range(n_pages):
        pg = pt_ref[b, j]
        for src, buf, sem, axis in zip(srcs, bufs, sems, token_axes):
            window = pl.ds(j * PAGE, PAGE)
            dst = buf.at[slot, window] if axis == 0 else buf.at[slot, :, window]
            out.append(pltpu.make_async_copy(src.at[pg], dst, sem.at[slot]))
    return out


def _paged_prefetch(pt_ref, n_pages, srcs, bufs, sems, token_axes, issue_early):
    b = pl.program_id(0)
    nb = pl.num_programs(0)
    slot = b % 2
    copies = functools.partial(_page_copies, pt_ref, n_pages=n_pages, srcs=srcs, bufs=bufs, sems=sems,
                               token_axes=token_axes)

    @pl.when(b == 0)
    def _():
        for c in copies(b=0, slot=0):
            c.start()

    if issue_early:
        @pl.when(b + 1 < nb)
        def _():
            for c in copies(b=b + 1, slot=1 - slot):
                c.start()

    for c in copies(b=b, slot=slot):
        c.wait()
    if issue_early:
        return slot, lambda: None

    nxt = jnp.minimum(b + 1, nb - 1)
    for c in copies(b=nxt, slot=1 - slot):
        c.start()

    def drain():
        @pl.when(b == nb - 1)
        def _():
            for c in copies(b=nxt, slot=1 - slot):
                c.wait()

    return slot, drain


def _mla_decode_kernel(pt_ref, q_ref, qabs_ref, knew_ref, cnew_ref, gkr_ref, wukt_ref, cos_ref, sin_ref,
                       ckv_hbm, kr_hbm, o_ref, ckv_buf, kr_buf, s_scr, sem_c, sem_r, *, n_pages, chunk):
    t_past = n_pages * PAGE
    slot, drain = _paged_prefetch(pt_ref, n_pages, (ckv_hbm, kr_hbm), (ckv_buf, kr_buf), (sem_c, sem_r), (0, 1),
                                  issue_early=False)

    q = q_ref[0].astype(F32)
    s_new = jnp.sum(q * knew_ref[0].astype(F32), axis=-1, keepdims=True)
    lane = lax.broadcasted_iota(jnp.int32, (MLA_HEADS, HEAD_PAD), 1)
    first = (lane >= MLA_NOPE) & (lane < MLA_NOPE + 16)
    second = (lane >= MLA_NOPE + 16) & (lane < MLA_QK)
    psi2 = jnp.where(first, pltpu.roll(q, HEAD_PAD - 16, 1), 0.0) - jnp.where(second, pltpu.roll(q, 16, 1), 0.0)
    gkr = gkr_ref[...]
    psi1 = (q * gkr)[:, MLA_NOPE:MLA_QK].astype(BF16)
    psi2 = (psi2 * gkr)[:, MLA_NOPE:MLA_QK].astype(BF16)
    lhs = jnp.concatenate([wukt_ref[...], qabs_ref[0]], axis=0)
    n_up = MLA_HEADS * MLA_NOPE

    for c in range(t_past // chunk):
        start = c * chunk
        ckv_c = ckv_buf[slot, pl.ds(start, chunk), :].astype(BF16)
        r = _dot_nt(lhs, ckv_c)
        kn = r[:n_up]
        k2 = jnp.sum((kn * kn).reshape(MLA_HEADS, MLA_NOPE, chunk), axis=1)
        kr_c = kr_buf[slot, :, pl.ds(start, chunk)]
        cos_c = cos_ref[:, pl.ds(start, chunk)]
        sin_c = sin_ref[:, pl.ds(start, chunk)]
        s_rope = _dot(psi1, (kr_c * cos_c).astype(BF16)) + _dot(psi2, (kr_c * sin_c).astype(BF16))
        kr2 = jnp.sum(kr_c * kr_c, axis=0, keepdims=True)
        inv = lax.rsqrt((k2 + kr2) * (1.0 / MLA_QK) + EPS)
        s_scr[:, pl.ds(start, chunk)] = (r[n_up:n_up + MLA_HEADS] + s_rope) * inv
    s = s_scr[...]
    m = jnp.maximum(jnp.max(s, axis=-1, keepdims=True), s_new)
    p = jnp.exp2(s - m)
    p_new = jnp.exp2(s_new - m)
    denom = jnp.sum(p, axis=-1, keepdims=True) + p_new
    s_scr[...] = p

    acc = jnp.zeros((MLA_HEADS, MLA_KV_LORA), F32)
    for c in range(t_past // chunk):
        start = c * chunk
        ckv_c = ckv_buf[slot, pl.ds(start, chunk), :].astype(BF16)
        acc = acc + _dot(s_scr[:, pl.ds(start, chunk)].astype(BF16), ckv_c)
    o_ref[0] = (acc + p_new * cnew_ref[0]) / denom
    drain()


def _mla_decode(page_table, q, qabs16, knew, cnew, gkr, wukt2, cos32, sin32, cache_ckv, cache_kr, chunk):
    nb, n_pages = page_table.shape
    t_past = n_pages * PAGE
    kern = functools.partial(_mla_decode_kernel, n_pages=n_pages, chunk=chunk)
    b3 = lambda b, pt: (b, 0, 0)
    grid_spec = pltpu.PrefetchScalarGridSpec(
        num_scalar_prefetch=1, grid=(nb,),
        in_specs=[
            pl.BlockSpec((1, MLA_HEADS, HEAD_PAD), b3), pl.BlockSpec((1, 16, MLA_KV_LORA), b3),
            pl.BlockSpec((1, MLA_HEADS, HEAD_PAD), b3), pl.BlockSpec((1, 1, MLA_KV_LORA), b3),
            _const_spec((1, HEAD_PAD)), _const_spec((MLA_HEADS * MLA_NOPE, MLA_KV_LORA)),
            _const_spec((MLA_ROPE, t_past)), _const_spec((MLA_ROPE, t_past)),
            pl.BlockSpec(memory_space=pl.ANY), pl.BlockSpec(memory_space=pl.ANY),
        ],
        out_specs=pl.BlockSpec((1, MLA_HEADS, MLA_KV_LORA), b3),
        scratch_shapes=[
            pltpu.VMEM((2, t_past, MLA_KV_LORA), F32), pltpu.VMEM((2, MLA_ROPE, t_past), F32),
            pltpu.VMEM((MLA_HEADS, t_past), F32),
            pltpu.SemaphoreType.DMA((2,)), pltpu.SemaphoreType.DMA((2,)),
        ],
    )
    return pl.pallas_call(
        kern, grid_spec=grid_spec,
        out_shape=jax.ShapeDtypeStruct((nb, MLA_HEADS, MLA_KV_LORA), F32),
        compiler_params=_cparams("arbitrary"), name="mla_decode",
    )(page_table, q, qabs16, knew, cnew, gkr, wukt2, cos32, sin32, cache_ckv, cache_kr)


def _out_ffn_kernel(*refs, n_parts, latent):
    parts = refs[:n_parts]
    x_ref = refs[n_parts]
    idx = n_parts + 1
    if latent:
        wuv_ref = refs[idx]
        idx += 1
    wouts = refs[idx:idx + n_parts]
    gffn_ref, wgu_ref, wd_ref, o_ref, h_scr = refs[idx + n_parts:]
    mix = jnp.zeros(x_ref.shape, F32)
    for p_ref, w_ref in zip(parts, wouts):
        if latent and p_ref is parts[-1]:
            for hd in range(MLA_HEADS):
                lat = p_ref[:, hd * MLA_KV_LORA:(hd + 1) * MLA_KV_LORA].astype(BF16)
                a = _dot(lat, wuv_ref[hd]).astype(BF16)
                mix = mix + _dot(a, w_ref[hd * MLA_V:(hd + 1) * MLA_V, :])
        else:
            mix = mix + _dot(p_ref[...].astype(BF16), w_ref[...])
    x1 = x_ref[...] + mix
    h_scr[...] = _rms(x1, gffn_ref[...]).astype(BF16)
    acc = jnp.zeros_like(x1)
    for c in range(FFN_HIDDEN // FFN_CHUNK):
        lo = c * FFN_CHUNK
        gate = _dot(h_scr[...], wgu_ref[:, lo:lo + FFN_CHUNK])
        up = _dot(h_scr[...], wgu_ref[:, FFN_HIDDEN + lo:FFN_HIDDEN + lo + FFN_CHUNK])
        act = (gate / (1.0 + jnp.exp(-gate)) * up).astype(BF16)
        acc = acc + _dot(act, wd_ref[lo:lo + FFN_CHUNK, :])
    o_ref[...] = x1 + acc


def _out_ffn(parts, x, wouts, gffn, wgu, wd, tm, wuv=None):
    m = x.shape[0]
    row = lambda i: (i, 0)
    in_specs = [pl.BlockSpec((tm, p.shape[1]), row) for p in parts] + [pl.BlockSpec((tm, D_MODEL), row)]
    args = list(parts) + [x]
    if wuv is not None:
        in_specs.append(_const_spec(wuv.shape))
        args.append(wuv)
    in_specs += [_const_spec(w.shape) for w in wouts]
    in_specs += [_const_spec((1, D_MODEL)), _const_spec(wgu.shape), _const_spec(wd.shape)]
    args += list(wouts) + [gffn, wgu, wd]
    return pl.pallas_call(
        functools.partial(_out_ffn_kernel, n_parts=len(parts), latent=wuv is not None),
        grid=(m // tm,), in_specs=in_specs,
        out_specs=pl.BlockSpec((tm, D_MODEL), row),
        out_shape=jax.ShapeDtypeStruct((m, D_MODEL), F32),
        scratch_shapes=[pltpu.VMEM((tm, D_MODEL), BF16)],
        compiler_params=_cparams("parallel"), name="out_ffn",
    )(*args)


def _prep_c_kernel(x_ref, gmix_ref, wqkv_ref, gq_ref, gk_ref, seg_ref, ca_ref, cb1_ref, cb2_ref,
                   kf_ref, vf_ref, q_ref, k_ref, vt_ref):
    h = _rms(x_ref[...], gmix_ref[...]).astype(BF16)
    qkv = _dot(h, wqkv_ref[...])
    ca = ca_ref[...]
    cb1 = cb1_ref[...]
    cb2 = cb2_ref[...]
    seg = seg_ref[...]

    def norm_rope(t, g):
        ms = _dot((t * t).astype(BF16), seg) * (1.0 / C_HEAD_DIM)
        t = t * lax.rsqrt(ms + EPS) * g
        return t * ca + pltpu.roll(t, LANES - C_ROT // 2, 1) * cb1 + pltpu.roll(t, C_ROT // 2, 1) * cb2

    lane = lax.broadcasted_iota(jnp.int32, (1, LANES), 1)
    low = lane < C_HEAD_DIM
    for j in range(C_Q_WIDTH // LANES):
        qt = norm_rope(qkv[:, j * LANES:(j + 1) * LANES], gq_ref[...])
        q_ref[j] = (qt * (C_SCALE * LOG2E)).astype(BF16)
    low_row = lax.broadcasted_iota(jnp.int32, (LANES, x_ref.shape[0]), 0) < C_HEAD_DIM
    for j in range(C_KV_WIDTH // LANES):
        kt = norm_rope(qkv[:, C_Q_WIDTH + j * LANES:C_Q_WIDTH + (j + 1) * LANES], gk_ref[...])
        vt = qkv[:, C_Q_WIDTH + C_KV_WIDTH + j * LANES:C_Q_WIDTH + C_KV_WIDTH + (j + 1) * LANES]
        vt_t = vt.T
        kf_ref[0, j * LANES:(j + 1) * LANES, :] = kt.T
        vf_ref[0, j * LANES:(j + 1) * LANES, :] = vt_t
        k_ref[2 * j] = jnp.where(low, kt, 0.0).astype(BF16)
        k_ref[2 * j + 1] = jnp.where(low, 0.0, kt).astype(BF16)
        vt_ref[2 * j] = jnp.where(low_row, vt_t, 1.0).astype(BF16)
        vt_ref[2 * j + 1] = jnp.where(low_row, 1.0, vt_t).astype(BF16)


def _prep_c(x, gmix, wc, tabs, tm, seq):
    m = x.shape[0]
    n_tab = tabs[0].shape[0] // tm
    per_seq = seq // tm
    row = lambda i: (i, 0)
    tab = lambda i: (i % n_tab, 0)
    feat = lambda i: (i // per_seq, 0, i % per_seq)
    nq = C_Q_WIDTH // LANES
    return pl.pallas_call(
        _prep_c_kernel,
        grid=(m // tm,),
        in_specs=[
            pl.BlockSpec((tm, D_MODEL), row), _const_spec((1, D_MODEL)),
            _const_spec((D_MODEL, C_Q_WIDTH + 2 * C_KV_WIDTH)),
            _const_spec((1, LANES)), _const_spec((1, LANES)), _const_spec((LANES, LANES)),
            pl.BlockSpec((tm, LANES), tab), pl.BlockSpec((tm, LANES), tab), pl.BlockSpec((tm, LANES), tab),
        ],
        out_specs=[
            pl.BlockSpec((1, C_KV_WIDTH, tm), feat), pl.BlockSpec((1, C_KV_WIDTH, tm), feat),
            pl.BlockSpec((nq, tm, LANES), lambda i: (0, i, 0)),
            pl.BlockSpec((C_KV_HEADS, tm, LANES), lambda i: (0, i, 0)),
            pl.BlockSpec((C_KV_HEADS, LANES, tm), lambda i: (0, 0, i)),
        ],
        out_shape=[
            jax.ShapeDtypeStruct((m // seq, C_KV_WIDTH, seq), F32),
            jax.ShapeDtypeStruct((m // seq, C_KV_WIDTH, seq), F32),
            jax.ShapeDtypeStruct((nq, m, LANES), BF16),
            jax.ShapeDtypeStruct((C_KV_HEADS, m, LANES), BF16),
            jax.ShapeDtypeStruct((C_KV_HEADS, LANES, m), BF16),
        ],
        compiler_params=_cparams("parallel"), name="prep_c",
    )(x, gmix, wc["w_qkv"], wc["g_q"], wc["g_k"], wc["seg"], *tabs)


def _topk_rank_mask(g, valid, n_rows):
    row = lax.broadcasted_iota(jnp.int32, g.shape, 0)
    g = jnp.where(valid, g, NEG)
    cnt = jnp.zeros(g.shape, F32)
    for mrow in range(n_rows):
        gm = g[mrow:mrow + 1, :]
        cnt = cnt + jnp.where(row > mrow, jnp.where(gm >= g, 1.0, 0.0), jnp.where(gm > g, 1.0, 0.0))
    return jnp.where(valid & (cnt < MOBA_TOPK), 1.0, 0.0)


def _moba_prompt_kernel(q_ref, k_ref, vt_ref, o_ref, m_scr, acc_scr, bias_scr, *, n_blk):
    qi = pl.program_id(1)
    blk = MOBA_BLOCK
    nq = C_GROUP * blk
    causal = (lax.broadcasted_iota(jnp.int32, (blk, nq), 0)
              <= (lax.broadcasted_iota(jnp.int32, (blk, nq), 1) & (blk - 1)))
    cand = lax.broadcasted_iota(jnp.int32, (8, nq), 0)
    cand_k = lax.broadcasted_iota(jnp.int32, (8, LANES), 0)

    def queries(kv):
        pair = kv // 2
        return q_ref[pair * C_GROUP:(pair + 1) * C_GROUP].reshape(nq, LANES)

    for kv in range(C_KV_HEADS):
        kmean = jnp.zeros((8, LANES), F32)
        for n in range(n_blk):
            ksum = jnp.sum(k_ref[kv, n * blk:(n + 1) * blk, :].astype(F32), axis=0, keepdims=True)
            kmean = jnp.where(cand_k == n, ksum * (1.0 / blk), kmean)
        k_hi = kmean.astype(BF16)
        k_lo = (kmean - k_hi.astype(F32)).astype(BF16)
        q = queries(kv)
        gate_t = _dot_nt(k_hi, q) + _dot_nt(k_lo, q)
        bias_scr[kv] = (_topk_rank_mask(gate_t, cand < qi, 8) - 1.0) * (-NEG)
    m_scr[...] = jnp.full(m_scr.shape, NEG, F32)
    acc_scr[...] = jnp.zeros(acc_scr.shape, F32)

    def block(keys, add_rows, mask):
        for kv in range(C_KV_HEADS):
            add = None if add_rows is None else [bias_scr[kv, n:n + 1, :] for n in add_rows]
            m_new, acc = _flash_t_step(k_ref[kv, keys, :], queries(kv), vt_ref[kv, :, keys],
                                       m_scr[kv:kv + 1, :], acc_scr[kv], add=add, mask=mask)
            m_scr[kv:kv + 1, :] = m_new
            acc_scr[kv] = acc

    for n in range(0, n_blk - 1, 2):
        if n + 2 <= n_blk - 1:
            @pl.when(n + 1 < qi)
            def _():
                block(slice(n * blk, (n + 2) * blk), (n, n + 1), None)

        @pl.when(n + 1 == qi)
        def _():
            block(slice(n * blk, (n + 1) * blk), (n,), None)

    block(pl.ds(pl.multiple_of(qi * blk, blk), blk), None, causal)

    low = lax.broadcasted_iota(jnp.int32, (LANES, nq), 0) < C_HEAD_DIM
    for pair in range(C_KV_HEADS // 2):
        even = acc_scr[2 * pair]
        odd = acc_scr[2 * pair + 1]
        o_t = jnp.where(low, even / even[C_HEAD_DIM:C_HEAD_DIM + 1, :], odd / odd[0:1, :])
        for g in range(C_GROUP):
            tile = pair * C_GROUP + g
            o_ref[:, tile * LANES:(tile + 1) * LANES] = o_t[:, g * blk:(g + 1) * blk].T.astype(BF16)


def _moba_prompt(q, k, vt, seq):
    nq_t, m, _ = q.shape
    n_blk = seq // MOBA_BLOCK
    assert n_blk <= 8
    nq = C_GROUP * MOBA_BLOCK
    return pl.pallas_call(
        functools.partial(_moba_prompt_kernel, n_blk=n_blk),
        grid=(m // seq, n_blk),
        in_specs=[
            pl.BlockSpec((nq_t, MOBA_BLOCK, LANES), lambda b, i: (0, b * n_blk + i, 0)),
            pl.BlockSpec((C_KV_HEADS, seq, LANES), lambda b, i: (0, b, 0)),
            pl.BlockSpec((C_KV_HEADS, LANES, seq), lambda b, i: (0, 0, b)),
        ],
        out_specs=pl.BlockSpec((MOBA_BLOCK, nq_t * LANES), lambda b, i: (b * n_blk + i, 0)),
        out_shape=jax.ShapeDtypeStruct((m, nq_t * LANES), BF16),
        scratch_shapes=[pltpu.VMEM((C_KV_HEADS, nq), F32), pltpu.VMEM((C_KV_HEADS, LANES, nq), F32),
                        pltpu.VMEM((C_KV_HEADS, 8, nq), F32)],
        compiler_params=_cparams("parallel", "arbitrary"), name="moba_prompt",
    )(q, k, vt)


def _moba_decode_kernel(pt_ref, qbd_ref, knew_ref, vnew_ref, k_hbm, v_hbm, o_ref,
                        k_buf, v_buf, s_scr, sem_k, sem_v, *, n_pages, chunk):
    t_past = n_pages * PAGE
    n_blk = t_past // MOBA_BLOCK
    slot, drain = _paged_prefetch(pt_ref, n_pages, (k_hbm, v_hbm), (k_buf, v_buf), (sem_k, sem_v), (1, 1),
                                  issue_early=True)
    qbd = qbd_ref[0]
    s_new = jnp.sum(qbd.astype(F32) * knew_ref[0], axis=-1, keepdims=True)

    for c in range(t_past // chunk):
        cols = pl.ds(c * chunk, chunk)
        s_scr[:, cols] = _dot(qbd, k_buf[slot, :, cols].astype(BF16))

    gates, maxes = [], []
    for n in range(n_blk):
        s_n = s_scr[:, n * MOBA_BLOCK:(n + 1) * MOBA_BLOCK]
        gates.append(jnp.sum(s_n, axis=-1, keepdims=True))
        maxes.append(jnp.max(s_n, axis=-1, keepdims=True))
    sels = []
    for n in range(n_blk):
        cnt = jnp.zeros_like(gates[n])
        for mb in range(n_blk):
            if mb != n:
                beats = (gates[mb] >= gates[n]) if mb < n else (gates[mb] > gates[n])
                cnt = cnt + jnp.where(beats, 1.0, 0.0)
        sels.append(cnt < MOBA_TOPK)
    m = s_new
    for n in range(n_blk):
        m = jnp.maximum(m, jnp.where(sels[n], maxes[n], NEG))
    p_new = jnp.exp2(s_new - m)
    psum = jnp.zeros((C_HEADS, MOBA_BLOCK), F32)
    for n in range(n_blk):
        sl = slice(n * MOBA_BLOCK, (n + 1) * MOBA_BLOCK)
        p = jnp.where(sels[n], jnp.exp2(s_scr[:, sl] - m), 0.0)
        s_scr[:, sl] = p
        psum = psum + p
    denom = jnp.sum(psum, axis=-1, keepdims=True) + p_new

    acc = jnp.zeros((C_HEADS, C_KV_WIDTH), F32)
    for c in range(t_past // chunk):
        cols = pl.ds(c * chunk, chunk)
        acc = acc + _dot_nt(s_scr[:, cols].astype(BF16), v_buf[slot, :, cols].astype(BF16))
    o_ref[0] = (acc + p_new * vnew_ref[0]) / denom
    drain()


def _moba_decode(page_table, qbd, knew, vnew, cache_k, cache_v, chunk):
    nb, n_pages = page_table.shape
    t_past = n_pages * PAGE
    b3 = lambda b, pt: (b, 0, 0)
    grid_spec = pltpu.PrefetchScalarGridSpec(
        num_scalar_prefetch=1, grid=(nb,),
        in_specs=[
            pl.BlockSpec((1, C_HEADS, C_KV_WIDTH), b3), pl.BlockSpec((1, 1, C_KV_WIDTH), b3),
            pl.BlockSpec((1, 1, C_KV_WIDTH), b3),
            pl.BlockSpec(memory_space=pl.ANY), pl.BlockSpec(memory_space=pl.ANY),
        ],
        out_specs=pl.BlockSpec((1, C_HEADS, C_KV_WIDTH), b3),
        scratch_shapes=[
            pltpu.VMEM((2, C_KV_WIDTH, t_past), F32), pltpu.VMEM((2, C_KV_WIDTH, t_past), F32),
            pltpu.VMEM((C_HEADS, t_past), F32),
            pltpu.SemaphoreType.DMA((2,)), pltpu.SemaphoreType.DMA((2,)),
        ],
    )
    return pl.pallas_call(
        functools.partial(_moba_decode_kernel, n_pages=n_pages, chunk=chunk), grid_spec=grid_spec,
        out_shape=jax.ShapeDtypeStruct((nb, C_HEADS, C_KV_WIDTH), F32),
        compiler_params=_cparams("arbitrary"), name="moba_decode",
    )(page_table, qbd, knew, vnew, cache_k, cache_v)


def _rope_angles(pos, dim, theta):
    inv_freq = jnp.exp(jnp.arange(0, dim, 2, dtype=F32) * (-math.log(theta) / dim))
    ang = pos.astype(F32)[:, None] * inv_freq[None, :]
    return jnp.cos(ang), jnp.sin(ang)


def _mla_tables(pos):
    cos, sin = _rope_angles(pos, MLA_ROPE, MLA_THETA)
    z = lambda n: jnp.zeros((pos.shape[0], n), F32)
    half = MLA_ROPE // 2
    tail = HEAD_PAD - MLA_QK
    cos_t = jnp.concatenate([z(MLA_NOPE), cos, cos, z(tail)], axis=1)
    sina = jnp.concatenate([z(MLA_NOPE), -sin, z(half), z(tail)], axis=1)
    sinb = jnp.concatenate([z(MLA_NOPE), z(half), sin, z(tail)], axis=1)
    return cos_t, sina, sinb


def _moba_tables(pos):
    cos, sin = _rope_angles(pos, C_ROT, C_THETA)
    n = pos.shape[0]
    half = C_ROT // 2
    rest = C_HEAD_DIM - C_ROT
    one = jnp.ones((n, rest), F32)
    z = lambda k: jnp.zeros((n, k), F32)
    ca = jnp.concatenate([cos, cos, one] * 2, axis=1)
    cb1 = jnp.concatenate([-sin, z(half), z(rest)] * 2, axis=1)
    cb2 = jnp.concatenate([z(half), sin, z(rest)] * 2, axis=1)
    return ca, cb1, cb2


def _pad_lanes(a, start, total):
    pad = [(0, 0)] * (a.ndim - 1) + [(start, total - start - a.shape[-1])]
    return jnp.pad(a, pad)


def _layer_a_weights(w_in, g_q_lora, g_kv_lora, w_uq, w_uk, w_uv, g_q, g_k, w_pool, s_pool, w_out):
    o_kr = POOL_WIDTH + MLA_Q_LORA + MLA_KV_LORA
    w_in_p = jnp.concatenate([w_in[:, :o_kr], _pad_lanes(w_in[:, o_kr:], MLA_NOPE, HEAD_PAD)], axis=1)
    hp = MLA_HEADS * HEAD_PAD
    return {
        "w_in": w_in_p.astype(BF16),
        "g_q_lora": g_q_lora[None], "g_kv_lora": g_kv_lora[None],
        "w_uq": _pad_lanes(w_uq, 0, HEAD_PAD).reshape(MLA_Q_LORA, hp).astype(BF16),
        "w_uk": _pad_lanes(w_uk, 0, HEAD_PAD).reshape(MLA_KV_LORA, hp).astype(BF16),
        "w_uv": _pad_lanes(w_uv, 0, HEAD_PAD).reshape(MLA_KV_LORA, hp).astype(BF16),
        "w_uk_t": _pad_lanes(jnp.transpose(w_uk, (1, 0, 2)), 0, HEAD_PAD).transpose(0, 2, 1).astype(BF16),
        "w_uk_t2": jnp.transpose(w_uk, (1, 2, 0)).reshape(MLA_HEADS * MLA_NOPE, MLA_KV_LORA).astype(BF16),
        "w_uv_h": jnp.transpose(w_uv, (1, 0, 2)).astype(BF16),
        "g_q": _pad_lanes(g_q, 0, HEAD_PAD)[None],
        "g_kn": _pad_lanes(g_k[:MLA_NOPE], 0, HEAD_PAD)[None],
        "g_kr": _pad_lanes(g_k[MLA_NOPE:], MLA_NOPE, HEAD_PAD)[None],
        "w_pool": w_pool.astype(BF16), "s_pool": s_pool[None],
        "w_out_pool": w_out[:POOL_WIDTH].astype(BF16), "w_out_mla": w_out[POOL_WIDTH:].astype(BF16),
    }


def _moba_tile_heads():
    order = []
    for j in range(C_Q_WIDTH // LANES):
        p, g = divmod(j, C_GROUP)
        order += [(2 * p) * C_GROUP + g, (2 * p + 1) * C_GROUP + g]
    return order


def _layer_c_weights(w_qkv, g_q, g_k, w_o):
    order = jnp.array(_moba_tile_heads())
    wq = w_qkv[:, :C_Q_WIDTH].reshape(D_MODEL, C_HEADS, C_HEAD_DIM)[:, order].reshape(D_MODEL, C_Q_WIDTH)
    w_o_p = w_o.reshape(C_HEADS, C_HEAD_DIM, D_MODEL)[order].reshape(C_Q_WIDTH, D_MODEL)
    seg = jnp.kron(jnp.eye(LANES // C_HEAD_DIM, dtype=F32), jnp.ones((C_HEAD_DIM, C_HEAD_DIM), F32))
    return {
        "w_qkv": jnp.concatenate([wq, w_qkv[:, C_Q_WIDTH:]], axis=1).astype(BF16),
        "g_q": jnp.tile(g_q, 2)[None], "g_k": jnp.tile(g_k, 2)[None], "seg": seg.astype(BF16),
        "w_o": w_o_p.astype(BF16),
    }


def _pages_feature_major(cache):
    n = cache.shape[0]
    return jnp.transpose(cache, (0, 2, 3, 1)).reshape(n, C_KV_WIDTH, PAGE)


def _tiles_to_rows(t):
    return jnp.transpose(t, (1, 0, 2)).reshape(t.shape[1], t.shape[0] * LANES)


def kernel(x_prompt, x_sample, cache_mla_ckv, cache_mla_krope, state_pool, cache_moba_k, cache_moba_v,
           page_table, g_mix, g_ffn, w_in_a, g_q_lora, g_kv_lora, w_uq, w_uk, w_uv, g_mla_q, g_mla_k,
           w_pool, s_pool, w_out_a, w_qkv_c, g_moba_q, g_moba_k, w_o_c, w_gate_up, w_down):
    bp, seq, _ = x_prompt.shape
    bs = x_sample.shape[0]
    n_pages = page_table.shape[1]
    past = n_pages * PAGE
    depth = g_mix.shape[0]
    mp = bp * seq
    xp = x_prompt.reshape(mp, D_MODEL)
    xs = x_sample.reshape(bs, D_MODEL)
    tm_p = tm_f = next(t for t in TOKEN_TILES if seq % t == 0)
    tq = next(t for t in MLA_TILES if seq % t == 0)
    chunk_a = min(MLA_DECODE_CHUNK, past)
    chunk_c = min(MOBA_DECODE_CHUNK, past)

    pos_p = jnp.arange(seq, dtype=jnp.int32)
    pos_s = jnp.full((bs,), past, dtype=jnp.int32)
    wgu = w_gate_up.astype(BF16)
    wdn = w_down.astype(BF16)

    outs_p = {k: [] for k in ("ckv", "kr", "pool", "k", "v")}
    outs_s = {k: [] for k in ("ckv", "kr", "pool", "k", "v")}
    for layer in range(depth):
        i = layer // 2
        gmix = g_mix[layer][None]
        gffn = g_ffn[layer][None]
        if layer % 2 == 0:
            wa = _layer_a_weights(w_in_a[i], g_q_lora[i], g_kv_lora[i], w_uq[i], w_uk[i], w_uv[i],
                                  g_mla_q[i], g_mla_k[i], w_pool[i], s_pool[i], w_out_a[i])
            u, ckv, kr, q, k, v = _prep_a(xp, gmix, wa, _mla_tables(pos_p), tm_p, decode=False)
            pool_out = _pool_prompt(u, wa["w_pool"], wa["s_pool"], seq, tm_p)
            mla_out = _mla_prompt(q, k, v, seq, tq, tq)
            xp = _out_ffn([pool_out, mla_out], xp, [wa["w_out_pool"], wa["w_out_mla"]], gffn,
                          wgu[layer], wdn[layer], tm_f)
            outs_p["ckv"].append(ckv.reshape(bp, seq, MLA_KV_LORA))
            outs_p["kr"].append(kr.reshape(bp, seq, MLA_ROPE))
            outs_p["pool"].append(u.reshape(bp, seq, POOL_WIDTH)[:, seq - POOL_STATE:])
            u, ckv, kr, q, k, v, qabs = _prep_a(xs, gmix, wa, _mla_tables(pos_s), bs, decode=True)
            state = state_pool[i]
            pool_out = _pool_sample(jnp.transpose(state, (1, 0, 2)), u, wa["w_pool"], wa["s_pool"])
            cos_k, sin_k = _rope_angles(jnp.arange(past, dtype=jnp.int32), MLA_ROPE, MLA_THETA)
            qabs16 = jnp.pad(qabs.reshape(bs, MLA_HEADS, MLA_KV_LORA), ((0, 0), (0, 16 - MLA_HEADS), (0, 0)))
            o_lat = _mla_decode(
                page_table, q.reshape(bs, MLA_HEADS, HEAD_PAD), qabs16, k.reshape(bs, MLA_HEADS, HEAD_PAD),
                ckv.reshape(bs, 1, MLA_KV_LORA), wa["g_kr"], wa["w_uk_t2"],
                jnp.concatenate([cos_k, cos_k], axis=1).T, jnp.concatenate([sin_k, sin_k], axis=1).T,
                cache_mla_ckv[i], jnp.transpose(cache_mla_krope[i], (0, 2, 1)), chunk_a)
            xs = _out_ffn([pool_out, o_lat.reshape(bs, MLA_HEADS * MLA_KV_LORA)], xs,
                          [wa["w_out_pool"], wa["w_out_mla"]], gffn, wgu[layer], wdn[layer], bs,
                          wuv=wa["w_uv_h"])
            outs_s["ckv"].append(ckv.reshape(bs, 1, MLA_KV_LORA))
            outs_s["kr"].append(kr.reshape(bs, 1, MLA_ROPE))
            outs_s["pool"].append(jnp.concatenate([state[:, 1:], u[:, None]], axis=1))
        else:
            wc = _layer_c_weights(w_qkv_c[i], g_moba_q[i], g_moba_k[i], w_o_c[i])
            kf, vf, q, k, v = _prep_c(xp, gmix, wc, _moba_tables(pos_p), tm_p, seq)
            o = _moba_prompt(q, k, v, seq)
            xp = _out_ffn([o], xp, [wc["w_o"]], gffn, wgu[layer], wdn[layer], tm_f)
            token_major = lambda t: jnp.transpose(t.reshape(bp, C_KV_HEADS, C_HEAD_DIM, seq), (0, 3, 1, 2))
            outs_p["k"].append(token_major(kf))
            outs_p["v"].append(token_major(vf))
            kf, vf, q, k, v = _prep_c(xs, gmix, wc, _moba_tables(pos_s), bs, bs)
            kf, vf = kf[0].T, vf[0].T
            qh = _tiles_to_rows(q).reshape(bs, C_Q_WIDTH // LANES, 2, C_HEAD_DIM)
            kv_of = jnp.array(_moba_tile_heads()).reshape(-1, 2) // C_GROUP
            onehot = (kv_of[:, :, None] == jnp.arange(C_KV_HEADS)[None, None, :]).astype(BF16)
            qbd = (qh[:, :, :, None, :] * onehot[None, :, :, :, None]).reshape(bs, C_HEADS, C_KV_WIDTH)
            o = _moba_decode(page_table, qbd, kf.reshape(bs, 1, C_KV_WIDTH), vf.reshape(bs, 1, C_KV_WIDTH),
                             _pages_feature_major(cache_moba_k[i]), _pages_feature_major(cache_moba_v[i]), chunk_c)
            o = (o.reshape(bs, C_Q_WIDTH // LANES, 2, C_KV_HEADS, C_HEAD_DIM)
                 * onehot[None, :, :, :, None].astype(F32)).sum(axis=3).reshape(bs, C_Q_WIDTH)
            xs = _out_ffn([o], xs, [wc["w_o"]], gffn, wgu[layer], wdn[layer], bs)
            outs_s["k"].append(kf.reshape(bs, 1, C_KV_HEADS, C_HEAD_DIM))
            outs_s["v"].append(vf.reshape(bs, 1, C_KV_HEADS, C_HEAD_DIM))

    st = lambda xs_: jnp.stack(xs_)
    return (xp.reshape(bp, seq, D_MODEL), xs.reshape(bs, 1, D_MODEL),
            st(outs_p["ckv"]), st(outs_p["kr"]), st(outs_p["pool"]), st(outs_p["k"]), st(outs_p["v"]),
            st(outs_s["ckv"]), st(outs_s["kr"]), st(outs_s["pool"]), st(outs_s["k"]), st(outs_s["v"]))
```

```python
import functools
import math

import jax
import jax.numpy as jnp
from jax import lax
from jax.experimental import pallas as pl
from jax.experimental.pallas import tpu as pltpu

F32 = jnp.float32
BF16 = jnp.bfloat16
LOG2E = math.log2(math.e)

D_MODEL = 1024
EPS = 1e-6
NEG = -1e30

POOL_WIDTH = 512
POOL_GROUPS = 4
POOL_GROUP_DIM = 128
POOL_WINDOWS = (2, 4, 8, 16)
POOL_STATE = 15
POOL_HALO = 16

MLA_HEADS = 8
MLA_NOPE = 64
MLA_ROPE = 32
MLA_QK = 96
MLA_V = 64
MLA_Q_LORA = 384
MLA_KV_LORA = 256
MLA_THETA = 10000.0
MLA_SCALE = MLA_QK ** -0.5
HEAD_PAD = 128
MLA_TILES = (1024, 512, 256)
A_IN_PAD = POOL_WIDTH + MLA_Q_LORA + MLA_KV_LORA + HEAD_PAD

C_HEADS = 16
C_KV_HEADS = 4
C_GROUP = 4
C_HEAD_DIM = 64
C_ROT = 16
C_THETA = 500000.0
C_SCALE = C_HEAD_DIM ** -0.5
MOBA_BLOCK = 256
MOBA_TOPK = 3
C_Q_WIDTH = C_HEADS * C_HEAD_DIM
C_KV_WIDTH = C_KV_HEADS * C_HEAD_DIM

FFN_HIDDEN = 2816
FFN_CHUNK = 256

PAGE = 128
LANES = 128
TOKEN_TILES = (1024, 512, 256)
MLA_DECODE_CHUNK = 4096
MOBA_DECODE_CHUNK = 8192
VMEM_LIMIT = 56 * 1024 * 1024


def _cparams(*sem):
    return pltpu.CompilerParams(dimension_semantics=sem, vmem_limit_bytes=VMEM_LIMIT)


def _const_spec(shape):
    nd = len(shape)
    return pl.BlockSpec(shape, lambda *_: (0,) * nd, pipeline_mode=pl.Buffered(1))


def _rms(x, g):
    ms = jnp.mean(x * x, axis=-1, keepdims=True)
    return x * lax.rsqrt(ms + EPS) * g


def _dot(a, b):
    return jnp.dot(a, b, preferred_element_type=F32)


def _dot_nt(a, b):
    return lax.dot_general(a, b, (((1,), (1,)), ((), ())), preferred_element_type=F32)


def _prep_a_kernel(x_ref, gmix_ref, win_ref, gql_ref, gkvl_ref, wuq_ref, gq_ref, gkn_ref, gkr_ref,
                   wuk_ref, wuv_ref, cos_ref, sina_ref, sinb_ref, *rest, decode):
    if decode:
        wukt_ref, u_ref, ckv_ref, kr_ref, q_ref, k_ref, vt_ref, qabs_ref = rest
    else:
        u_ref, ckv_ref, kr_ref, q_ref, k_ref, vt_ref = rest
    h = _rms(x_ref[...], gmix_ref[...]).astype(BF16)
    hw = _dot(h, win_ref[...])
    u_ref[...] = hw[:, :POOL_WIDTH]
    o_q = POOL_WIDTH
    o_kv = o_q + MLA_Q_LORA
    o_kr = o_kv + MLA_KV_LORA
    cq = _rms(hw[:, o_q:o_kv], gql_ref[...]).astype(BF16)
    ckv = _rms(hw[:, o_kv:o_kr], gkvl_ref[...])
    ckv_ref[...] = ckv
    kr = hw[:, o_kr:o_kr + HEAD_PAD]
    kr_ref[...] = kr[:, MLA_NOPE:MLA_QK]
    cos = cos_ref[...]
    sina = sina_ref[...]
    sinb = sinb_ref[...]

    def rope(t, base):
        return t * base + pltpu.roll(t, HEAD_PAD - 16, 1) * sina + pltpu.roll(t, 16, 1) * sinb

    ones = jnp.ones((HEAD_PAD, HEAD_PAD), BF16)

    def sum_sq(t):
        return _dot((t * t).astype(BF16), ones)

    q = _dot(cq, wuq_ref[...])
    gq = gq_ref[...]
    qbase = gkn_ref[...] + cos
    for hd in range(MLA_HEADS):
        sl = slice(hd * HEAD_PAD, (hd + 1) * HEAD_PAD)
        blk = q[:, sl]
        ms = sum_sq(blk) * (1.0 / MLA_QK)
        qt = (rope(blk * lax.rsqrt(ms + EPS) * gq, qbase) * (MLA_SCALE * LOG2E)).astype(BF16)
        q_ref[:, sl] = qt
        if decode:
            qabs_ref[:, hd * MLA_KV_LORA:(hd + 1) * MLA_KV_LORA] = _dot(qt, wukt_ref[hd]).astype(BF16)

    ckv_b = ckv.astype(BF16)
    kn = _dot(ckv_b, wuk_ref[...])
    krsq = sum_sq(kr)
    krot = rope(kr * gkr_ref[...], cos)
    for hd in range(MLA_HEADS):
        sl = slice(hd * HEAD_PAD, (hd + 1) * HEAD_PAD)
        blk = kn[:, sl]
        ms = (sum_sq(blk) + krsq) * (1.0 / MLA_QK)
        k_ref[:, sl] = ((blk + krot) * lax.rsqrt(ms + EPS)).astype(BF16)
    lane = lax.broadcasted_iota(jnp.int32, (1, HEAD_PAD), 1)
    vv = _dot(ckv_b, wuv_ref[...])
    for hd in range(MLA_HEADS):
        vt_ref[hd] = jnp.where(lane < MLA_V, vv[:, hd * HEAD_PAD:(hd + 1) * HEAD_PAD], 1.0).T.astype(BF16)


def _prep_a(x, gmix, wa, tabs, tm, decode):
    m = x.shape[0]
    n_tab = tabs[0].shape[0] // tm
    row = lambda i: (i, 0)
    tab = lambda i: (i % n_tab, 0)
    hp = MLA_HEADS * HEAD_PAD
    in_specs = [
        pl.BlockSpec((tm, D_MODEL), row), _const_spec((1, D_MODEL)), _const_spec((D_MODEL, A_IN_PAD)),
        _const_spec((1, MLA_Q_LORA)), _const_spec((1, MLA_KV_LORA)), _const_spec((MLA_Q_LORA, hp)),
        _const_spec((1, HEAD_PAD)), _const_spec((1, HEAD_PAD)), _const_spec((1, HEAD_PAD)),
        _const_spec((MLA_KV_LORA, hp)), _const_spec((MLA_KV_LORA, hp)),
        pl.BlockSpec((tm, HEAD_PAD), tab), pl.BlockSpec((tm, HEAD_PAD), tab), pl.BlockSpec((tm, HEAD_PAD), tab),
    ]
    args = [x, gmix, wa["w_in"], wa["g_q_lora"], wa["g_kv_lora"], wa["w_uq"], wa["g_q"], wa["g_kn"], wa["g_kr"],
            wa["w_uk"], wa["w_uv"], *tabs]
    out_shape = [
        jax.ShapeDtypeStruct((m, POOL_WIDTH), F32), jax.ShapeDtypeStruct((m, MLA_KV_LORA), F32),
        jax.ShapeDtypeStruct((m, MLA_ROPE), F32), jax.ShapeDtypeStruct((m, hp), BF16),
        jax.ShapeDtypeStruct((m, hp), BF16), jax.ShapeDtypeStruct((MLA_HEADS, HEAD_PAD, m), BF16),
    ]
    out_specs = [
        pl.BlockSpec((tm, POOL_WIDTH), row), pl.BlockSpec((tm, MLA_KV_LORA), row),
        pl.BlockSpec((tm, MLA_ROPE), row), pl.BlockSpec((tm, hp), row),
        pl.BlockSpec((tm, hp), row), pl.BlockSpec((MLA_HEADS, HEAD_PAD, tm), lambda i: (0, 0, i)),
    ]
    if decode:
        in_specs.append(_const_spec((MLA_HEADS, HEAD_PAD, MLA_KV_LORA)))
        args.append(wa["w_uk_t"])
        out_shape.append(jax.ShapeDtypeStruct((m, MLA_HEADS * MLA_KV_LORA), BF16))
        out_specs.append(pl.BlockSpec((tm, MLA_HEADS * MLA_KV_LORA), row))
    return pl.pallas_call(
        functools.partial(_prep_a_kernel, decode=decode),
        grid=(m // tm,), in_specs=in_specs, out_specs=out_specs, out_shape=out_shape,
        compiler_params=_cparams("parallel"), name="prep_a_dec" if decode else "prep_a",
    )(*args)


def _pool_mix(diffs, wp_ref, sp_ref, o_ref):
    for g in range(POOL_GROUPS):
        sl = slice(g * POOL_GROUP_DIM, (g + 1) * POOL_GROUP_DIM)
        o_ref[:, sl] = _dot(diffs[g].astype(BF16), wp_ref[g]) * sp_ref[:, sl]


def _pool_prompt_kernel(u_ref, halo_ref, wp_ref, sp_ref, o_ref, ext_ref):
    j = pl.program_id(1)
    tm = u_ref.shape[0]
    ext_ref[:POOL_HALO, :] = jnp.where(j > 0, halo_ref[...], 0.0)
    ext_ref[POOL_HALO:, :] = u_ref[...]
    pos = j * tm + lax.broadcasted_iota(jnp.int32, (tm, POOL_GROUP_DIM), 0)
    diffs = []
    for g, w in enumerate(POOL_WINDOWS):
        sl = slice(g * POOL_GROUP_DIM, (g + 1) * POOL_GROUP_DIM)
        cur = ext_ref[POOL_HALO:, sl]
        acc = cur
        for k in range(1, w):
            acc = acc + ext_ref[POOL_HALO - k:POOL_HALO - k + tm, sl]
        cnt = jnp.minimum(pos + 1, w).astype(F32)
        diffs.append(acc / cnt - cur)
    _pool_mix(diffs, wp_ref, sp_ref, o_ref)


def _pool_prompt(u, wp, sp, seq, tm):
    m = u.shape[0]
    nj = seq // tm
    hb = tm // POOL_HALO
    return pl.pallas_call(
        _pool_prompt_kernel,
        grid=(m // seq, nj),
        in_specs=[
            pl.BlockSpec((tm, POOL_WIDTH), lambda b, j: (b * nj + j, 0)),
            pl.BlockSpec((POOL_HALO, POOL_WIDTH), lambda b, j: (jnp.maximum((b * nj + j) * hb - 1, 0), 0)),
            _const_spec((POOL_GROUPS, POOL_GROUP_DIM, POOL_GROUP_DIM)), _const_spec((1, POOL_WIDTH)),
        ],
        out_specs=pl.BlockSpec((tm, POOL_WIDTH), lambda b, j: (b * nj + j, 0)),
        out_shape=jax.ShapeDtypeStruct((m, POOL_WIDTH), F32),
        scratch_shapes=[pltpu.VMEM((tm + POOL_HALO, POOL_WIDTH), F32)],
        compiler_params=_cparams("parallel", "parallel"), name="pool_prompt",
    )(u, u, wp, sp)


def _pool_sample_kernel(st_ref, u_ref, wp_ref, sp_ref, o_ref):
    diffs = []
    for g, w in enumerate(POOL_WINDOWS):
        sl = slice(g * POOL_GROUP_DIM, (g + 1) * POOL_GROUP_DIM)
        cur = u_ref[:, sl]
        acc = cur
        for k in range(1, w):
            acc = acc + st_ref[POOL_STATE - k, :, sl]
        diffs.append(acc / float(w) - cur)
    _pool_mix(diffs, wp_ref, sp_ref, o_ref)


def _pool_sample(state_t, u, wp, sp):
    m = u.shape[0]
    return pl.pallas_call(
        _pool_sample_kernel,
        grid=(1,),
        in_specs=[_const_spec(state_t.shape), _const_spec(u.shape),
                  _const_spec((POOL_GROUPS, POOL_GROUP_DIM, POOL_GROUP_DIM)), _const_spec((1, POOL_WIDTH))],
        out_specs=_const_spec((m, POOL_WIDTH)),
        out_shape=jax.ShapeDtypeStruct((m, POOL_WIDTH), F32),
        compiler_params=_cparams("arbitrary"), name="pool_sample",
    )(state_t, u, wp, sp)


def _flash_t_step(k_blk, q, vt_blk, m_prev, acc_prev, add=None, mask=None):
    s = _dot_nt(k_blk, q)
    if add is not None:
        step = k_blk.shape[0] // len(add)
        s = jnp.concatenate([s[j * step:(j + 1) * step] + row for j, row in enumerate(add)], axis=0)
    if mask is not None:
        s = jnp.where(mask, s, NEG)
    m_new = jnp.maximum(m_prev, jnp.max(s, axis=0, keepdims=True))
    p = jnp.exp2(s - m_new)
    acc = jnp.exp2(m_prev - m_new) * acc_prev + _dot(vt_blk, p.astype(BF16))
    return m_new, acc


def _mla_prompt_kernel(q_ref, k_ref, vt_ref, o_ref, m_scr, acc_scr, *, tk):
    qi = pl.program_id(1)
    tq = q_ref.shape[0]
    per_q = tq // tk
    key_in_tile = lax.broadcasted_iota(jnp.int32, (tk, tq), 0)
    query_in_tile = lax.broadcasted_iota(jnp.int32, (tk, tq), 1)
    m_scr[...] = jnp.full(m_scr.shape, NEG, F32)
    acc_scr[...] = jnp.zeros(acc_scr.shape, F32)

    def block(kb, mask):
        start = pl.multiple_of(kb * tk, tk)
        for hd in range(MLA_HEADS):
            sl = slice(hd * HEAD_PAD, (hd + 1) * HEAD_PAD)
            m_new, acc = _flash_t_step(k_ref[pl.ds(start, tk), sl], q_ref[:, sl], vt_ref[hd, :, pl.ds(start, tk)],
                                       m_scr[hd:hd + 1, :], acc_scr[hd], mask=mask)
            m_scr[hd:hd + 1, :] = m_new
            acc_scr[hd] = acc

    def past(kb, carry):
        block(kb, None)
        return carry

    lax.fori_loop(0, qi * per_q, past, 0)
    for j in range(per_q):
        block(qi * per_q + j, key_in_tile + j * tk <= query_in_tile)
    for hd in range(MLA_HEADS):
        acc = acc_scr[hd]
        o = (acc / acc[MLA_V:MLA_V + 1, :]).T
        o_ref[:, hd * MLA_V:(hd + 1) * MLA_V] = o[:, :MLA_V]


def _mla_prompt(q, k, vt, seq, tq, tk):
    m = q.shape[0]
    nq = seq // tq
    hp = MLA_HEADS * HEAD_PAD
    return pl.pallas_call(
        functools.partial(_mla_prompt_kernel, tk=tk),
        grid=(m // seq, nq),
        in_specs=[
            pl.BlockSpec((tq, hp), lambda b, i: (b * nq + i, 0)),
            pl.BlockSpec((seq, hp), lambda b, i: (b, 0)),
            pl.BlockSpec((MLA_HEADS, HEAD_PAD, seq), lambda b, i: (0, 0, b)),
        ],
        out_specs=pl.BlockSpec((tq, MLA_HEADS * MLA_V), lambda b, i: (b * nq + i, 0)),
        out_shape=jax.ShapeDtypeStruct((m, MLA_HEADS * MLA_V), F32),
        scratch_shapes=[pltpu.VMEM((MLA_HEADS, tq), F32), pltpu.VMEM((MLA_HEADS, HEAD_PAD, tq), F32)],
        compiler_params=_cparams("parallel", "arbitrary"), name="mla_prompt",
    )(q, k, vt)


def _page_copies(pt_ref, b, n_pages, srcs, bufs, sems, token_axes, slot):
    out = []
    for j in range(n_pages):
        pg = pt_ref[b, j]
        for src, buf, sem, axis in zip(srcs, bufs, sems, token_axes):
            window = pl.ds(j * PAGE, PAGE)
            dst = buf.at[slot, window] if axis == 0 else buf.at[slot, :, window]
            out.append(pltpu.make_async_copy(src.at[pg], dst, sem.at[slot]))
    return out


def _paged_prefetch(pt_ref, n_pages, srcs, bufs, sems, token_axes, issue_early):
    b = pl.program_id(0)
    nb = pl.num_programs(0)
    slot = b % 2
    copies = functools.partial(_page_copies, pt_ref, n_pages=n_pages, srcs=srcs, bufs=bufs, sems=sems,
                               token_axes=token_axes)

    def start_all(cs):
        for i, c in enumerate(cs):
            c.start(priority=0 if issue_early else i % len(srcs))

    @pl.when(b == 0)
    def _():
        start_all(copies(b=0, slot=0))

    if issue_early:
        @pl.when(b + 1 < nb)
        def _():
            start_all(copies(b=b + 1, slot=1 - slot))

    for c in copies(b=b, slot=slot):
        c.wait()
    if issue_early:
        return slot, lambda: None

    nxt = jnp.minimum(b + 1, nb - 1)
    start_all(copies(b=nxt, slot=1 - slot))

    def drain():
        @pl.when(b == nb - 1)
        def _():
            for c in copies(b=nxt, slot=1 - slot):
                c.wait()

    return slot, drain


def _mla_decode_kernel(pt_ref, q_ref, qabs_ref, knew_ref, cnew_ref, gkr_ref, wukt_ref, cos_ref, sin_ref,
                       ckv_hbm, kr_hbm, o_ref, ckv_buf, kr_buf, s_scr, sem_c, sem_r, *, n_pages, chunk):
    t_past = n_pages * PAGE
    slot, drain = _paged_prefetch(pt_ref, n_pages, (ckv_hbm, kr_hbm), (ckv_buf, kr_buf), (sem_c, sem_r), (0, 1),
                                  issue_early=False)

    q = q_ref[0].astype(F32)
    s_new = jnp.sum(q * knew_ref[0].astype(F32), axis=-1, keepdims=True)
    lane = lax.broadcasted_iota(jnp.int32, (MLA_HEADS, HEAD_PAD), 1)
    first = (lane >= MLA_NOPE) & (lane < MLA_NOPE + 16)
    second = (lane >= MLA_NOPE + 16) & (lane < MLA_QK)
    psi2 = jnp.where(first, pltpu.roll(q, HEAD_PAD - 16, 1), 0.0) - jnp.where(second, pltpu.roll(q, 16, 1), 0.0)
    gkr = gkr_ref[...]
    psi1 = (q * gkr)[:, MLA_NOPE:MLA_QK].astype(BF16)
    psi2 = (psi2 * gkr)[:, MLA_NOPE:MLA_QK].astype(BF16)
    lhs = jnp.concatenate([wukt_ref[...], qabs_ref[0]], axis=0)
    n_up = MLA_HEADS * MLA_NOPE

    for c in range(t_past // chunk):
        start = c * chunk
        ckv_c = ckv_buf[slot, pl.ds(start, chunk), :].astype(BF16)
        r = _dot_nt(lhs, ckv_c)
        kn = r[:n_up]
        k2 = jnp.sum((kn * kn).reshape(MLA_HEADS, MLA_NOPE, chunk), axis=1)
        kr_c = kr_buf[slot, :, pl.ds(start, chunk)]
        cos_c = cos_ref[:, pl.ds(start, chunk)]
        sin_c = sin_ref[:, pl.ds(start, chunk)]
        s_rope = _dot(psi1, (kr_c * cos_c).astype(BF16)) + _dot(psi2, (kr_c * sin_c).astype(BF16))
        kr2 = jnp.sum(kr_c * kr_c, axis=0, keepdims=True)
        inv = lax.rsqrt((k2 + kr2) * (1.0 / MLA_QK) + EPS)
        s_scr[:, pl.ds(start, chunk)] = (r[n_up:n_up + MLA_HEADS] + s_rope) * inv
    s = s_scr[...]
    m = jnp.maximum(jnp.max(s, axis=-1, keepdims=True), s_new)
    p = jnp.exp2(s - m)
    p_new = jnp.exp2(s_new - m)
    denom = jnp.sum(p, axis=-1, keepdims=True) + p_new
    s_scr[...] = p

    acc = jnp.zeros((MLA_HEADS, MLA_KV_LORA), F32)
    for c in range(t_past // chunk):
        start = c * chunk
        ckv_c = ckv_buf[slot, pl.ds(start, chunk), :].astype(BF16)
        acc = acc + _dot(s_scr[:, pl.ds(start, chunk)].astype(BF16), ckv_c)
    o_ref[0] = (acc + p_new * cnew_ref[0]) / denom
    drain()


def _mla_decode(page_table, q, qabs16, knew, cnew, gkr, wukt2, cos32, sin32, cache_ckv, cache_kr, chunk):
    nb, n_pages = page_table.shape
    t_past = n_pages * PAGE
    kern = functools.partial(_mla_decode_kernel, n_pages=n_pages, chunk=chunk)
    b3 = lambda b, pt: (b, 0, 0)
    grid_spec = pltpu.PrefetchScalarGridSpec(
        num_scalar_prefetch=1, grid=(nb,),
        in_specs=[
            pl.BlockSpec((1, MLA_HEADS, HEAD_PAD), b3), pl.BlockSpec((1, 16, MLA_KV_LORA), b3),
            pl.BlockSpec((1, MLA_HEADS, HEAD_PAD), b3), pl.BlockSpec((1, 1, MLA_KV_LORA), b3),
            _const_spec((1, HEAD_PAD)), _const_spec((MLA_HEADS * MLA_NOPE, MLA_KV_LORA)),
            _const_spec((MLA_ROPE, t_past)), _const_spec((MLA_ROPE, t_past)),
            pl.BlockSpec(memory_space=pl.ANY), pl.BlockSpec(memory_space=pl.ANY),
        ],
        out_specs=pl.BlockSpec((1, MLA_HEADS, MLA_KV_LORA), b3),
        scratch_shapes=[
            pltpu.VMEM((2, t_past, MLA_KV_LORA), F32), pltpu.VMEM((2, MLA_ROPE, t_past), F32),
            pltpu.VMEM((MLA_HEADS, t_past), F32),
            pltpu.SemaphoreType.DMA((2,)), pltpu.SemaphoreType.DMA((2,)),
        ],
    )
    return pl.pallas_call(
        kern, grid_spec=grid_spec,
        out_shape=jax.ShapeDtypeStruct((nb, MLA_HEADS, MLA_KV_LORA), F32),
        compiler_params=_cparams("arbitrary"), name="mla_decode",
    )(page_table, q, qabs16, knew, cnew, gkr, wukt2, cos32, sin32, cache_ckv, cache_kr)


def _out_ffn_kernel(*refs, n_parts, latent):
    parts = refs[:n_parts]
    x_ref = refs[n_parts]
    idx = n_parts + 1
    if latent:
        wuv_ref = refs[idx]
        idx += 1
    wouts = refs[idx:idx + n_parts]
    gffn_ref, wgu_ref, wd_ref, o_ref, h_scr = refs[idx + n_parts:]
    mix = jnp.zeros(x_ref.shape, F32)
    for p_ref, w_ref in zip(parts, wouts):
        if latent and p_ref is parts[-1]:
            for hd in range(MLA_HEADS):
                lat = p_ref[:, hd * MLA_KV_LORA:(hd + 1) * MLA_KV_LORA].astype(BF16)
                a = _dot(lat, wuv_ref[hd]).astype(BF16)
                mix = mix + _dot(a, w_ref[hd * MLA_V:(hd + 1) * MLA_V, :])
        else:
            mix = mix + _dot(p_ref[...].astype(BF16), w_ref[...])
    x1 = x_ref[...] + mix
    h_scr[...] = _rms(x1, gffn_ref[...]).astype(BF16)
    acc = jnp.zeros_like(x1)
    for c in range(FFN_HIDDEN // FFN_CHUNK):
        lo = c * FFN_CHUNK
        gate = _dot(h_scr[...], wgu_ref[:, lo:lo + FFN_CHUNK])
        up = _dot(h_scr[...], wgu_ref[:, FFN_HIDDEN + lo:FFN_HIDDEN + lo + FFN_CHUNK])
        act = (gate / (1.0 + jnp.exp(-gate)) * up).astype(BF16)
        acc = acc + _dot(act, wd_ref[lo:lo + FFN_CHUNK, :])
    o_ref[...] = x1 + acc


def _out_ffn(parts, x, wouts, gffn, wgu, wd, tm, wuv=None):
    m = x.shape[0]
    row = lambda i: (i, 0)
    in_specs = [pl.BlockSpec((tm, p.shape[1]), row) for p in parts] + [pl.BlockSpec((tm, D_MODEL), row)]
    args = list(parts) + [x]
    if wuv is not None:
        in_specs.append(_const_spec(wuv.shape))
        args.append(wuv)
    in_specs += [_const_spec(w.shape) for w in wouts]
    in_specs += [_const_spec((1, D_MODEL)), _const_spec(wgu.shape), _const_spec(wd.shape)]
    args += list(wouts) + [gffn, wgu, wd]
    return pl.pallas_call(
        functools.partial(_out_ffn_kernel, n_parts=len(parts), latent=wuv is not None),
        grid=(m // tm,), in_specs=in_specs,
        out_specs=pl.BlockSpec((tm, D_MODEL), row),
        out_shape=jax.ShapeDtypeStruct((m, D_MODEL), F32),
        scratch_shapes=[pltpu.VMEM((tm, D_MODEL), BF16)],
        compiler_params=_cparams("parallel"), name="out_ffn",
    )(*args)


def _prep_c_kernel(x_ref, gmix_ref, wqkv_ref, gq_ref, gk_ref, seg_ref, ca_ref, cb1_ref, cb2_ref,
                   kf_ref, vf_ref, q_ref, k_ref, vt_ref):
    h = _rms(x_ref[...], gmix_ref[...]).astype(BF16)
    qkv = _dot(h, wqkv_ref[...])
    ca = ca_ref[...]
    cb1 = cb1_ref[...]
    cb2 = cb2_ref[...]
    seg = seg_ref[...]

    def norm_rope(t, g):
        ms = _dot((t * t).astype(BF16), seg) * (1.0 / C_HEAD_DIM)
        t = t * lax.rsqrt(ms + EPS) * g
        return t * ca + pltpu.roll(t, LANES - C_ROT // 2, 1) * cb1 + pltpu.roll(t, C_ROT // 2, 1) * cb2

    lane = lax.broadcasted_iota(jnp.int32, (1, LANES), 1)
    low = lane < C_HEAD_DIM
    for j in range(C_Q_WIDTH // LANES):
        qt = norm_rope(qkv[:, j * LANES:(j + 1) * LANES], gq_ref[...])
        q_ref[j] = (qt * (C_SCALE * LOG2E)).astype(BF16)
    low_row = lax.broadcasted_iota(jnp.int32, (LANES, x_ref.shape[0]), 0) < C_HEAD_DIM
    for j in range(C_KV_WIDTH // LANES):
        kt = norm_rope(qkv[:, C_Q_WIDTH + j * LANES:C_Q_WIDTH + (j + 1) * LANES], gk_ref[...])
        vt = qkv[:, C_Q_WIDTH + C_KV_WIDTH + j * LANES:C_Q_WIDTH + C_KV_WIDTH + (j + 1) * LANES]
        vt_t = vt.T
        kf_ref[0, j * LANES:(j + 1) * LANES, :] = kt.T
        vf_ref[0, j * LANES:(j + 1) * LANES, :] = vt_t
        k_ref[2 * j] = jnp.where(low, kt, 0.0).astype(BF16)
        k_ref[2 * j + 1] = jnp.where(low, 0.0, kt).astype(BF16)
        vt_ref[2 * j] = jnp.where(low_row, vt_t, 1.0).astype(BF16)
        vt_ref[2 * j + 1] = jnp.where(low_row, 1.0, vt_t).astype(BF16)


def _prep_c(x, gmix, wc, tabs, tm, seq):
    m = x.shape[0]
    n_tab = tabs[0].shape[0] // tm
    per_seq = seq // tm
    row = lambda i: (i, 0)
    tab = lambda i: (i % n_tab, 0)
    feat = lambda i: (i // per_seq, 0, i % per_seq)
    nq = C_Q_WIDTH // LANES
    return pl.pallas_call(
        _prep_c_kernel,
        grid=(m // tm,),
        in_specs=[
            pl.BlockSpec((tm, D_MODEL), row), _const_spec((1, D_MODEL)),
            _const_spec((D_MODEL, C_Q_WIDTH + 2 * C_KV_WIDTH)),
            _const_spec((1, LANES)), _const_spec((1, LANES)), _const_spec((LANES, LANES)),
            pl.BlockSpec((tm, LANES), tab), pl.BlockSpec((tm, LANES), tab), pl.BlockSpec((tm, LANES), tab),
        ],
        out_specs=[
            pl.BlockSpec((1, C_KV_WIDTH, tm), feat), pl.BlockSpec((1, C_KV_WIDTH, tm), feat),
            pl.BlockSpec((nq, tm, LANES), lambda i: (0, i, 0)),
            pl.BlockSpec((C_KV_HEADS, tm, LANES), lambda i: (0, i, 0)),
            pl.BlockSpec((C_KV_HEADS, LANES, tm), lambda i: (0, 0, i)),
        ],
        out_shape=[
            jax.ShapeDtypeStruct((m // seq, C_KV_WIDTH, seq), F32),
            jax.ShapeDtypeStruct((m // seq, C_KV_WIDTH, seq), F32),
            jax.ShapeDtypeStruct((nq, m, LANES), BF16),
            jax.ShapeDtypeStruct((C_KV_HEADS, m, LANES), BF16),
            jax.ShapeDtypeStruct((C_KV_HEADS, LANES, m), BF16),
        ],
        compiler_params=_cparams("parallel"), name="prep_c",
    )(x, gmix, wc["w_qkv"], wc["g_q"], wc["g_k"], wc["seg"], *tabs)


def _topk_rank_mask(g, valid, n_rows):
    row = lax.broadcasted_iota(jnp.int32, g.shape, 0)
    g = jnp.where(valid, g, NEG)
    cnt = jnp.zeros(g.shape, F32)
    for mrow in range(n_rows):
        gm = g[mrow:mrow + 1, :]
        cnt = cnt + jnp.where(row > mrow, jnp.where(gm >= g, 1.0, 0.0), jnp.where(gm > g, 1.0, 0.0))
    return jnp.where(valid & (cnt < MOBA_TOPK), 1.0, 0.0)


def _moba_prompt_kernel(q_ref, k_ref, vt_ref, o_ref, m_scr, acc_scr, bias_scr, *, n_blk):
    qi = pl.program_id(1)
    blk = MOBA_BLOCK
    nq = C_GROUP * blk
    causal = (lax.broadcasted_iota(jnp.int32, (blk, nq), 0)
              <= (lax.broadcasted_iota(jnp.int32, (blk, nq), 1) & (blk - 1)))
    cand = lax.broadcasted_iota(jnp.int32, (8, nq), 0)
    cand_k = lax.broadcasted_iota(jnp.int32, (8, LANES), 0)

    def queries(kv):
        pair = kv // 2
        return q_ref[pair * C_GROUP:(pair + 1) * C_GROUP].reshape(nq, LANES)

    for kv in range(C_KV_HEADS):
        kmean = jnp.zeros((8, LANES), F32)
        for n in range(n_blk):
            ksum = jnp.sum(k_ref[kv, n * blk:(n + 1) * blk, :].astype(F32), axis=0, keepdims=True)
            kmean = jnp.where(cand_k == n, ksum * (1.0 / blk), kmean)
        k_hi = kmean.astype(BF16)
        k_lo = (kmean - k_hi.astype(F32)).astype(BF16)
        q = queries(kv)
        gate_t = _dot_nt(k_hi, q) + _dot_nt(k_lo, q)
        bias_scr[kv] = (_topk_rank_mask(gate_t, cand < qi, 8) - 1.0) * (-NEG)
    m_scr[...] = jnp.full(m_scr.shape, NEG, F32)
    acc_scr[...] = jnp.zeros(acc_scr.shape, F32)

    def block(keys, add_rows, mask):
        for kv in range(C_KV_HEADS):
            add = None if add_rows is None else [bias_scr[kv, n:n + 1, :] for n in add_rows]
            m_new, acc = _flash_t_step(k_ref[kv, keys, :], queries(kv), vt_ref[kv, :, keys],
                                       m_scr[kv:kv + 1, :], acc_scr[kv], add=add, mask=mask)
            m_scr[kv:kv + 1, :] = m_new
            acc_scr[kv] = acc

    for n in range(0, n_blk - 1, 2):
        if n + 2 <= n_blk - 1:
            @pl.when(n + 1 < qi)
            def _():
                block(slice(n * blk, (n + 2) * blk), (n, n + 1), None)

        @pl.when(n + 1 == qi)
        def _():
            block(slice(n * blk, (n + 1) * blk), (n,), None)

    block(pl.ds(pl.multiple_of(qi * blk, blk), blk), None, causal)

    low = lax.broadcasted_iota(jnp.int32, (LANES, nq), 0) < C_HEAD_DIM
    for pair in range(C_KV_HEADS // 2):
        even = acc_scr[2 * pair]
        odd = acc_scr[2 * pair + 1]
        o_t = jnp.where(low, even / even[C_HEAD_DIM:C_HEAD_DIM + 1, :], odd / odd[0:1, :])
        for g in range(C_GROUP):
            tile = pair * C_GROUP + g
            o_ref[:, tile * LANES:(tile + 1) * LANES] = o_t[:, g * blk:(g + 1) * blk].T.astype(BF16)


def _moba_prompt(q, k, vt, seq):
    nq_t, m, _ = q.shape
    n_blk = seq // MOBA_BLOCK
    assert n_blk <= 8
    nq = C_GROUP * MOBA_BLOCK
    return pl.pallas_call(
        functools.partial(_moba_prompt_kernel, n_blk=n_blk),
        grid=(m // seq, n_blk),
        in_specs=[
            pl.BlockSpec((nq_t, MOBA_BLOCK, LANES), lambda b, i: (0, b * n_blk + i, 0)),
            pl.BlockSpec((C_KV_HEADS, seq, LANES), lambda b, i: (0, b, 0)),
            pl.BlockSpec((C_KV_HEADS, LANES, seq), lambda b, i: (0, 0, b)),
        ],
        out_specs=pl.BlockSpec((MOBA_BLOCK, nq_t * LANES), lambda b, i: (b * n_blk + i, 0)),
        out_shape=jax.ShapeDtypeStruct((m, nq_t * LANES), BF16),
        scratch_shapes=[pltpu.VMEM((C_KV_HEADS, nq), F32), pltpu.VMEM((C_KV_HEADS, LANES, nq), F32),
                        pltpu.VMEM((C_KV_HEADS, 8, nq), F32)],
        compiler_params=_cparams("parallel", "arbitrary"), name="moba_prompt",
    )(q, k, vt)


def _moba_decode_kernel(pt_ref, qbd_ref, knew_ref, vnew_ref, k_hbm, v_hbm, o_ref,
                        k_buf, v_buf, s_scr, sem_k, sem_v, *, n_pages, chunk):
    t_past = n_pages * PAGE
    n_blk = t_past // MOBA_BLOCK
    slot, drain = _paged_prefetch(pt_ref, n_pages, (k_hbm, v_hbm), (k_buf, v_buf), (sem_k, sem_v), (1, 1),
                                  issue_early=True)
    qbd = qbd_ref[0]
    s_new = jnp.sum(qbd.astype(F32) * knew_ref[0], axis=-1, keepdims=True)

    for c in range(t_past // chunk):
        cols = pl.ds(c * chunk, chunk)
        s_scr[:, cols] = _dot(qbd, k_buf[slot, :, cols].astype(BF16))

    gates, maxes = [], []
    for n in range(n_blk):
        s_n = s_scr[:, n * MOBA_BLOCK:(n + 1) * MOBA_BLOCK]
        gates.append(jnp.sum(s_n, axis=-1, keepdims=True))
        maxes.append(jnp.max(s_n, axis=-1, keepdims=True))
    sels = []
    for n in range(n_blk):
        cnt = jnp.zeros_like(gates[n])
        for mb in range(n_blk):
            if mb != n:
                beats = (gates[mb] >= gates[n]) if mb < n else (gates[mb] > gates[n])
                cnt = cnt + jnp.where(beats, 1.0, 0.0)
        sels.append(cnt < MOBA_TOPK)
    m = s_new
    for n in range(n_blk):
        m = jnp.maximum(m, jnp.where(sels[n], maxes[n], NEG))
    p_new = jnp.exp2(s_new - m)
    psum = jnp.zeros((C_HEADS, MOBA_BLOCK), F32)
    for n in range(n_blk):
        sl = slice(n * MOBA_BLOCK, (n + 1) * MOBA_BLOCK)
        p = jnp.where(sels[n], jnp.exp2(s_scr[:, sl] - m), 0.0)
        s_scr[:, sl] = p
        psum = psum + p
    denom = jnp.sum(psum, axis=-1, keepdims=True) + p_new

    acc = jnp.zeros((C_HEADS, C_KV_WIDTH), F32)
    for c in range(t_past // chunk):
        cols = pl.ds(c * chunk, chunk)
        acc = acc + _dot_nt(s_scr[:, cols].astype(BF16), v_buf[slot, :, cols].astype(BF16))
    o_ref[0] = (acc + p_new * vnew_ref[0]) / denom
    drain()


def _moba_decode(page_table, qbd, knew, vnew, cache_k, cache_v, chunk):
    nb, n_pages = page_table.shape
    t_past = n_pages * PAGE
    b3 = lambda b, pt: (b, 0, 0)
    grid_spec = pltpu.PrefetchScalarGridSpec(
        num_scalar_prefetch=1, grid=(nb,),
        in_specs=[
            pl.BlockSpec((1, C_HEADS, C_KV_WIDTH), b3), pl.BlockSpec((1, 1, C_KV_WIDTH), b3),
            pl.BlockSpec((1, 1, C_KV_WIDTH), b3),
            pl.BlockSpec(memory_space=pl.ANY), pl.BlockSpec(memory_space=pl.ANY),
        ],
        out_specs=pl.BlockSpec((1, C_HEADS, C_KV_WIDTH), b3),
        scratch_shapes=[
            pltpu.VMEM((2, C_KV_WIDTH, t_past), F32), pltpu.VMEM((2, C_KV_WIDTH, t_past), F32),
            pltpu.VMEM((C_HEADS, t_past), F32),
            pltpu.SemaphoreType.DMA((2,)), pltpu.SemaphoreType.DMA((2,)),
        ],
    )
    return pl.pallas_call(
        functools.partial(_moba_decode_kernel, n_pages=n_pages, chunk=chunk), grid_spec=grid_spec,
        out_shape=jax.ShapeDtypeStruct((nb, C_HEADS, C_KV_WIDTH), F32),
        compiler_params=_cparams("arbitrary"), name="moba_decode",
    )(page_table, qbd, knew, vnew, cache_k, cache_v)


def _rope_angles(pos, dim, theta):
    inv_freq = jnp.exp(jnp.arange(0, dim, 2, dtype=F32) * (-math.log(theta) / dim))
    ang = pos.astype(F32)[:, None] * inv_freq[None, :]
    return jnp.cos(ang), jnp.sin(ang)


def _mla_tables(pos):
    cos, sin = _rope_angles(pos, MLA_ROPE, MLA_THETA)
    z = lambda n: jnp.zeros((pos.shape[0], n), F32)
    half = MLA_ROPE // 2
    tail = HEAD_PAD - MLA_QK
    cos_t = jnp.concatenate([z(MLA_NOPE), cos, cos, z(tail)], axis=1)
    sina = jnp.concatenate([z(MLA_NOPE), -sin, z(half), z(tail)], axis=1)
    sinb = jnp.concatenate([z(MLA_NOPE), z(half), sin, z(tail)], axis=1)
    return cos_t, sina, sinb


def _moba_tables(pos):
    cos, sin = _rope_angles(pos, C_ROT, C_THETA)
    n = pos.shape[0]
    half = C_ROT // 2
    rest = C_HEAD_DIM - C_ROT
    one = jnp.ones((n, rest), F32)
    z = lambda k: jnp.zeros((n, k), F32)
    ca = jnp.concatenate([cos, cos, one] * 2, axis=1)
    cb1 = jnp.concatenate([-sin, z(half), z(rest)] * 2, axis=1)
    cb2 = jnp.concatenate([z(half), sin, z(rest)] * 2, axis=1)
    return ca, cb1, cb2


def _pad_lanes(a, start, total):
    pad = [(0, 0)] * (a.ndim - 1) + [(start, total - start - a.shape[-1])]
    return jnp.pad(a, pad)


def _layer_a_weights(w_in, g_q_lora, g_kv_lora, w_uq, w_uk, w_uv, g_q, g_k, w_pool, s_pool, w_out):
    o_kr = POOL_WIDTH + MLA_Q_LORA + MLA_KV_LORA
    w_in_p = jnp.concatenate([w_in[:, :o_kr], _pad_lanes(w_in[:, o_kr:], MLA_NOPE, HEAD_PAD)], axis=1)
    hp = MLA_HEADS * HEAD_PAD
    return {
        "w_in": w_in_p.astype(BF16),
        "g_q_lora": g_q_lora[None], "g_kv_lora": g_kv_lora[None],
        "w_uq": _pad_lanes(w_uq, 0, HEAD_PAD).reshape(MLA_Q_LORA, hp).astype(BF16),
        "w_uk": _pad_lanes(w_uk, 0, HEAD_PAD).reshape(MLA_KV_LORA, hp).astype(BF16),
        "w_uv": _pad_lanes(w_uv, 0, HEAD_PAD).reshape(MLA_KV_LORA, hp).astype(BF16),
        "w_uk_t": _pad_lanes(jnp.transpose(w_uk, (1, 0, 2)), 0, HEAD_PAD).transpose(0, 2, 1).astype(BF16),
        "w_uk_t2": jnp.transpose(w_uk, (1, 2, 0)).reshape(MLA_HEADS * MLA_NOPE, MLA_KV_LORA).astype(BF16),
        "w_uv_h": jnp.transpose(w_uv, (1, 0, 2)).astype(BF16),
        "g_q": _pad_lanes(g_q, 0, HEAD_PAD)[None],
        "g_kn": _pad_lanes(g_k[:MLA_NOPE], 0, HEAD_PAD)[None],
        "g_kr": _pad_lanes(g_k[MLA_NOPE:], MLA_NOPE, HEAD_PAD)[None],
        "w_pool": w_pool.astype(BF16), "s_pool": s_pool[None],
        "w_out_pool": w_out[:POOL_WIDTH].astype(BF16), "w_out_mla": w_out[POOL_WIDTH:].astype(BF16),
    }


def _moba_tile_heads():
    order = []
    for j in range(C_Q_WIDTH // LANES):
        p, g = divmod(j, C_GROUP)
        order += [(2 * p) * C_GROUP + g, (2 * p + 1) * C_GROUP + g]
    return order


def _layer_c_weights(w_qkv, g_q, g_k, w_o):
    order = jnp.array(_moba_tile_heads())
    wq = w_qkv[:, :C_Q_WIDTH].reshape(D_MODEL, C_HEADS, C_HEAD_DIM)[:, order].reshape(D_MODEL, C_Q_WIDTH)
    w_o_p = w_o.reshape(C_HEADS, C_HEAD_DIM, D_MODEL)[order].reshape(C_Q_WIDTH, D_MODEL)
    seg = jnp.kron(jnp.eye(LANES // C_HEAD_DIM, dtype=F32), jnp.ones((C_HEAD_DIM, C_HEAD_DIM), F32))
    return {
        "w_qkv": jnp.concatenate([wq, w_qkv[:, C_Q_WIDTH:]], axis=1).astype(BF16),
        "g_q": jnp.tile(g_q, 2)[None], "g_k": jnp.tile(g_k, 2)[None], "seg": seg.astype(BF16),
        "w_o": w_o_p.astype(BF16),
    }


def _pages_feature_major(cache):
    n = cache.shape[0]
    return jnp.transpose(cache, (0, 2, 3, 1)).reshape(n, C_KV_WIDTH, PAGE)


def _tiles_to_rows(t):
    return jnp.transpose(t, (1, 0, 2)).reshape(t.shape[1], t.shape[0] * LANES)


def kernel(x_prompt, x_sample, cache_mla_ckv, cache_mla_krope, state_pool, cache_moba_k, cache_moba_v,
           page_table, g_mix, g_ffn, w_in_a, g_q_lora, g_kv_lora, w_uq, w_uk, w_uv, g_mla_q, g_mla_k,
           w_pool, s_pool, w_out_a, w_qkv_c, g_moba_q, g_moba_k, w_o_c, w_gate_up, w_down):
    bp, seq, _ = x_prompt.shape
    bs = x_sample.shape[0]
    n_pages = page_table.shape[1]
    past = n_pages * PAGE
    depth = g_mix.shape[0]
    mp = bp * seq
    xp = x_prompt.reshape(mp, D_MODEL)
    xs = x_sample.reshape(bs, D_MODEL)
    tm_p = tm_f = next(t for t in TOKEN_TILES if seq % t == 0)
    tq = next(t for t in MLA_TILES if seq % t == 0)
    chunk_a = min(MLA_DECODE_CHUNK, past)
    chunk_c = min(MOBA_DECODE_CHUNK, past)

    pos_p = jnp.arange(seq, dtype=jnp.int32)
    pos_s = jnp.full((bs,), past, dtype=jnp.int32)
    wgu = w_gate_up.astype(BF16)
    wdn = w_down.astype(BF16)

    outs_p = {k: [] for k in ("ckv", "kr", "pool", "k", "v")}
    outs_s = {k: [] for k in ("ckv", "kr", "pool", "k", "v")}
    for layer in range(depth):
        i = layer // 2
        gmix = g_mix[layer][None]
        gffn = g_ffn[layer][None]
        if layer % 2 == 0:
            wa = _layer_a_weights(w_in_a[i], g_q_lora[i], g_kv_lora[i], w_uq[i], w_uk[i], w_uv[i],
                                  g_mla_q[i], g_mla_k[i], w_pool[i], s_pool[i], w_out_a[i])
            u, ckv, kr, q, k, v = _prep_a(xp, gmix, wa, _mla_tables(pos_p), tm_p, decode=False)
            pool_out = _pool_prompt(u, wa["w_pool"], wa["s_pool"], seq, tm_p)
            mla_out = _mla_prompt(q, k, v, seq, tq, tq)
            xp = _out_ffn([pool_out, mla_out], xp, [wa["w_out_pool"], wa["w_out_mla"]], gffn,
                          wgu[layer], wdn[layer], tm_f)
            outs_p["ckv"].append(ckv.reshape(bp, seq, MLA_KV_LORA))
            outs_p["kr"].append(kr.reshape(bp, seq, MLA_ROPE))
            outs_p["pool"].append(u.reshape(bp, seq, POOL_WIDTH)[:, seq - POOL_STATE:])
            u, ckv, kr, q, k, v, qabs = _prep_a(xs, gmix, wa, _mla_tables(pos_s), bs, decode=True)
            state = state_pool[i]
            pool_out = _pool_sample(jnp.transpose(state, (1, 0, 2)), u, wa["w_pool"], wa["s_pool"])
            cos_k, sin_k = _rope_angles(jnp.arange(past, dtype=jnp.int32), MLA_ROPE, MLA_THETA)
            qabs16 = jnp.pad(qabs.reshape(bs, MLA_HEADS, MLA_KV_LORA), ((0, 0), (0, 16 - MLA_HEADS), (0, 0)))
            o_lat = _mla_decode(
                page_table, q.reshape(bs, MLA_HEADS, HEAD_PAD), qabs16, k.reshape(bs, MLA_HEADS, HEAD_PAD),
                ckv.reshape(bs, 1, MLA_KV_LORA), wa["g_kr"], wa["w_uk_t2"],
                jnp.concatenate([cos_k, cos_k], axis=1).T, jnp.concatenate([sin_k, sin_k], axis=1).T,
                cache_mla_ckv[i], jnp.transpose(cache_mla_krope[i], (0, 2, 1)), chunk_a)
            xs = _out_ffn([pool_out, o_lat.reshape(bs, MLA_HEADS * MLA_KV_LORA)], xs,
                          [wa["w_out_pool"], wa["w_out_mla"]], gffn, wgu[layer], wdn[layer], bs,
                          wuv=wa["w_uv_h"])
            outs_s["ckv"].append(ckv.reshape(bs, 1, MLA_KV_LORA))
            outs_s["kr"].append(kr.reshape(bs, 1, MLA_ROPE))
            outs_s["pool"].append(jnp.concatenate([state[:, 1:], u[:, None]], axis=1))
        else:
            wc = _layer_c_weights(w_qkv_c[i], g_moba_q[i], g_moba_k[i], w_o_c[i])
            kf, vf, q, k, v = _prep_c(xp, gmix, wc, _moba_tables(pos_p), tm_p, seq)
            o = _moba_prompt(q, k, v, seq)
            xp = _out_ffn([o], xp, [wc["w_o"]], gffn, wgu[layer], wdn[layer], tm_f)
            token_major = lambda t: jnp.transpose(t.reshape(bp, C_KV_HEADS, C_HEAD_DIM, seq), (0, 3, 1, 2))
            outs_p["k"].append(token_major(kf))
            outs_p["v"].append(token_major(vf))
            kf, vf, q, k, v = _prep_c(xs, gmix, wc, _moba_tables(pos_s), bs, bs)
            kf, vf = kf[0].T, vf[0].T
            qh = _tiles_to_rows(q).reshape(bs, C_Q_WIDTH // LANES, 2, C_HEAD_DIM)
            kv_of = jnp.array(_moba_tile_heads()).reshape(-1, 2) // C_GROUP
            onehot = (kv_of[:, :, None] == jnp.arange(C_KV_HEADS)[None, None, :]).astype(BF16)
            qbd = (qh[:, :, :, None, :] * onehot[None, :, :, :, None]).reshape(bs, C_HEADS, C_KV_WIDTH)
            o = _moba_decode(page_table, qbd, kf.reshape(bs, 1, C_KV_WIDTH), vf.reshape(bs, 1, C_KV_WIDTH),
                             _pages_feature_major(cache_moba_k[i]), _pages_feature_major(cache_moba_v[i]), chunk_c)
            o = (o.reshape(bs, C_Q_WIDTH // LANES, 2, C_KV_HEADS, C_HEAD_DIM)
                 * onehot[None, :, :, :, None].astype(F32)).sum(axis=3).reshape(bs, C_Q_WIDTH)
            xs = _out_ffn([o], xs, [wc["w_o"]], gffn, wgu[layer], wdn[layer], bs)
            outs_s["k"].append(kf.reshape(bs, 1, C_KV_HEADS, C_HEAD_DIM))
            outs_s["v"].append(vf.reshape(bs, 1, C_KV_HEADS, C_HEAD_DIM))

    st = lambda xs_: jnp.stack(xs_)
    return (xp.reshape(bp, seq, D_MODEL), xs.reshape(bs, 1, D_MODEL),
            st(outs_p["ckv"]), st(outs_p["kr"]), st(outs_p["pool"]), st(outs_p["k"]), st(outs_p["v"]),
            st(outs_s["ckv"]), st(outs_s["kr"]), st(outs_s["pool"]), st(outs_s["k"]), st(outs_s["v"]))
```
